```python
import math
import jax, jax.numpy as jnp
from jax import lax
import numpy as np

D_MODEL = 1024
BATCH = 8
SEQ = 4096
DEPTH = 1
DEC_BATCH = 32
DEC_SEQ = 4
PAST_LEN = 16384
PAGE_SIZE = 128

N_HEADS = 4
ATT_WIDTH = D_MODEL // 2
V_DIM = ATT_WIDTH // N_HEADS
HEAD_DIM = V_DIM // 2
POOL_WINDOWS = (2, 4, 8, 16)
N_POOL_GROUPS = len(POOL_WINDOWS)
POOL_WIDTH = D_MODEL - ATT_WIDTH
POOL_GROUP_DIM = POOL_WIDTH // N_POOL_GROUPS
POOL_BUF = max(POOL_WINDOWS) - 1
MIX_WIDTH = ATT_WIDTH + POOL_WIDTH
IN_WIDTH = 3 * ATT_WIDTH + POOL_WIDTH
N_MEM = 256
X_HEADS = 4
X_HEAD_DIM = 64
X_WIDTH = X_HEADS * X_HEAD_DIM
N_EXPERTS = 32
TOP_K = 4
D_FF = D_MODEL
SWIGLU_LIMIT = 7.0
SWIGLU_ALPHA = 1.702
Q_BLOCK = 128
E_BLOCK = 128
EPS = 1e-6

kernel_name = "hymba_diffattn_pool_memxattn_moe_step"


def rms_norm(x, g):
    xf = x.astype(jnp.float32)
    y = xf * lax.rsqrt(jnp.mean(xf * xf, axis=-1, keepdims=True) + EPS)
    return (y * g.astype(jnp.float32)).astype(x.dtype)


def alibi_slopes():
    return jnp.asarray([2.0 ** (-8.0 * (h + 1) / N_HEADS) for h in range(N_HEADS)], dtype=jnp.float32)


def diff_lambda(lam, lam_init):
    lf = lam.astype(jnp.float32)
    return jnp.exp(jnp.sum(lf[0] * lf[1])) - jnp.exp(jnp.sum(lf[2] * lf[3])) + lam_init


def project_mixer(h, g_mix, w_in, g_q, g_k):
    b, t, _ = h.shape
    u = rms_norm(h, g_mix) @ w_in
    q = rms_norm(u[..., :ATT_WIDTH].reshape(b, t, N_HEADS, 2, HEAD_DIM), g_q)
    k = rms_norm(u[..., ATT_WIDTH:2 * ATT_WIDTH].reshape(b, t, N_HEADS, 2, HEAD_DIM), g_k)
    v = u[..., 2 * ATT_WIDTH:3 * ATT_WIDTH].reshape(b, t, N_HEADS, V_DIM)
    up = u[..., 3 * ATT_WIDTH:]
    return q, k, v, up


def diff_attend(q, k, v, q_pos, k_pos, slopes, lam_val, lam_init, g_sub):
    s = jnp.einsum('bqhcd,bkhcd->bhcqk', q, k).astype(jnp.float32) * (HEAD_DIM ** -0.5)
    dist = q_pos[:, None] - k_pos[None, :]
    bias = slopes[:, None, None, None] * dist.astype(jnp.float32)
    s = jnp.where(dist >= 0, s - bias, -jnp.inf)
    p = jax.nn.softmax(s, axis=-1)
    a = p[:, :, 0] - lam_val * p[:, :, 1]
    o = jnp.einsum('bhqk,bkhe->bqhe', a.astype(v.dtype), v)
    return rms_norm(o, g_sub) * (1.0 - lam_init)


def pool_mix(u_ext, n_prev, w_pool, pool_scale):
    b, n_all, _ = u_ext.shape
    uf = u_ext.astype(jnp.float32).reshape(b, n_all, N_POOL_GROUPS, POOL_GROUP_DIM)
    cs = jnp.cumsum(uf, axis=1)
    idx = jnp.arange(n_all)
    means = []
    for g, w in enumerate(POOL_WINDOWS):
        c = cs[:, :, g]
        lag = jnp.pad(c, ((0, 0), (w, 0), (0, 0)))[:, :n_all]
        cnt = jnp.minimum(idx + 1, w).astype(jnp.float32)[None, :, None]
        means.append((c - lag) / cnt)
    pooled = jnp.stack(means, axis=2)
    d = (pooled - uf)[:, n_prev:].astype(u_ext.dtype)
    y = jnp.einsum('btgc,gcd->btgd', d, w_pool)
    return y.reshape(b, n_all - n_prev, POOL_WIDTH) * pool_scale


def mem_kv(mem, g_mem, w_xkv, g_xk):
    b, m, _ = mem.shape
    kv = rms_norm(mem, g_mem) @ w_xkv
    mk = rms_norm(kv[..., :X_WIDTH].reshape(b, m, X_HEADS, X_HEAD_DIM), g_xk)
    mv = kv[..., X_WIDTH:].reshape(b, m, X_HEADS, X_HEAD_DIM)
    return mk, mv


def cross_attend(h, g_xattn, mk, mv, w_xq, g_xq, w_xo):
    b, t, _ = h.shape
    q = rms_norm((rms_norm(h, g_xattn) @ w_xq).reshape(b, t, X_HEADS, X_HEAD_DIM), g_xq)
    s = jnp.einsum('bqhd,bmhd->bhqm', q, mk).astype(jnp.float32) * (X_HEAD_DIM ** -0.5)
    p = jax.nn.softmax(s, axis=-1)
    o = jnp.einsum('bhqm,bmhd->bqhd', p.astype(mv.dtype), mv).reshape(b, t, X_WIDTH)
    return h + o @ w_xo


def moe_ffn(h, g_ffn, w_router, b_router, w_gate_up, b_gate_up, w_down, b_down):
    b, t, d = h.shape
    n_tok = b * t
    m = rms_norm(h, g_ffn).reshape(n_tok, d)
    logits = (m @ w_router).astype(jnp.float32) + b_router.astype(jnp.float32)
    top_val, top_idx = lax.top_k(logits, TOP_K)
    gates = jax.nn.softmax(top_val, axis=-1)
    n_assign = n_tok * TOP_K
    flat_e = top_idx.reshape(-1)
    flat_tok = jnp.repeat(jnp.arange(n_tok, dtype=jnp.int32), TOP_K)
    order = jnp.argsort(flat_e)
    sorted_e = flat_e[order]
    counts = jnp.bincount(flat_e, length=N_EXPERTS)
    padded = (counts + E_BLOCK - 1) // E_BLOCK * E_BLOCK
    start = jnp.cumsum(counts) - counts
    pend = jnp.cumsum(padded)
    pstart = pend - padded
    dest_sorted = pstart[sorted_e] + (jnp.arange(n_assign) - start[sorted_e])
    n_blocks = -(-n_assign // E_BLOCK) + N_EXPERTS
    n_rows = n_blocks * E_BLOCK
    row_tok = jnp.full((n_rows,), n_tok, jnp.int32).at[dest_sorted].set(flat_tok[order])
    m_pad = jnp.concatenate([m, jnp.zeros((1, d), m.dtype)], axis=0)
    xs = m_pad[row_tok].reshape(n_blocks, E_BLOCK, d)
    block_e = jnp.minimum(jnp.searchsorted(pend, jnp.arange(n_blocks) * E_BLOCK, side='right'), N_EXPERTS - 1)

    def run_block(args):
        xb, e = args
        gu = xb @ w_gate_up[e] + b_gate_up[e]
        gate = jnp.minimum(gu[:, :D_FF], SWIGLU_LIMIT)
        up = jnp.clip(gu[:, D_FF:], -SWIGLU_LIMIT, SWIGLU_LIMIT)
        glu = gate * jax.nn.sigmoid(SWIGLU_ALPHA * gate)
        return ((up + 1.0) * glu) @ w_down[e] + b_down[e]

    y_rows = lax.map(run_block, (xs, block_e)).reshape(n_rows, d)
    dest = jnp.zeros((n_assign,), jnp.int32).at[order].set(dest_sorted)
    y_assign = y_rows[dest].reshape(n_tok, TOP_K, d)
    out = jnp.einsum('tk,tkd->td', gates.astype(y_assign.dtype), y_assign)
    return h + out.reshape(b, t, d)


def setup_inputs(seed: int = 0) -> dict:
    key = jax.random.key(seed)
    ks = jax.random.split(key, 40)
    f32 = jnp.float32

    def nrm(k, shape, scale):
        return jax.random.normal(k, shape, f32) * scale

    def gain(k, shape):
        return 1.0 + 0.1 * jax.random.normal(k, shape, f32)

    n_pages = PAST_LEN // PAGE_SIZE
    n_used = DEC_BATCH * n_pages
    n_phys = n_used + n_used // 4
    page_table = jax.random.permutation(ks[0], n_phys)[:n_used].astype(jnp.int32).reshape(DEC_BATCH, n_pages)
    return {
        'x_prompt': nrm(ks[1], (BATCH, SEQ, D_MODEL), 1.0),
        'x_sample': nrm(ks[2], (DEC_BATCH, DEC_SEQ, D_MODEL), 1.0),
        'mem_prompt': nrm(ks[3], (BATCH, N_MEM, D_MODEL), 1.0),
        'cache_k': nrm(ks[4], (DEPTH, n_phys, PAGE_SIZE, N_HEADS, V_DIM), 1.0),
        'cache_v': nrm(ks[5], (DEPTH, n_phys, PAGE_SIZE, N_HEADS, V_DIM), 1.0),
        'cache_mem_k': nrm(ks[6], (DEPTH, DEC_BATCH, N_MEM, X_HEADS, X_HEAD_DIM), 1.0),
        'cache_mem_v': nrm(ks[7], (DEPTH, DEC_BATCH, N_MEM, X_HEADS, X_HEAD_DIM), 1.0),
        'state_pool': nrm(ks[8], (DEPTH, DEC_BATCH, POOL_BUF, POOL_WIDTH), 1.0),
        'page_table': page_table,
        'g_mix': gain(ks[9], (DEPTH, D_MODEL)),
        'w_in': nrm(ks[10], (DEPTH, D_MODEL, IN_WIDTH), D_MODEL ** -0.5),
        'g_q': gain(ks[11], (DEPTH, 2, HEAD_DIM)),
        'g_k': gain(ks[12], (DEPTH, 2, HEAD_DIM)),
        'lam': nrm(ks[13], (DEPTH, 4, HEAD_DIM), 0.1),
        'g_sub': gain(ks[14], (DEPTH, V_DIM)),
        'w_pool': nrm(ks[15], (DEPTH, N_POOL_GROUPS, POOL_GROUP_DIM, POOL_GROUP_DIM), POOL_GROUP_DIM ** -0.5),
        'pool_scale': gain(ks[16], (DEPTH, POOL_WIDTH)),
        'w_out': nrm(ks[17], (DEPTH, MIX_WIDTH, D_MODEL), MIX_WIDTH ** -0.5),
        'g_xattn': gain(ks[18], (DEPTH, D_MODEL)),
        'g_mem': gain(ks[19], (DEPTH, D_MODEL)),
        'w_xq': nrm(ks[20], (DEPTH, D_MODEL, X_WIDTH), D_MODEL ** -0.5),
        'w_xkv': nrm(ks[21], (DEPTH, D_MODEL, 2 * X_WIDTH), D_MODEL ** -0.5),
        'g_xq': gain(ks[22], (DEPTH, X_HEAD_DIM)),
        'g_xk': gain(ks[23], (DEPTH, X_HEAD_DIM)),
        'w_xo': nrm(ks[24], (DEPTH, X_WIDTH, D_MODEL), X_WIDTH ** -0.5),
        'g_ffn': gain(ks[25], (DEPTH, D_MODEL)),
        'w_router': nrm(ks[26], (DEPTH, D_MODEL, N_EXPERTS), D_MODEL ** -0.5),
        'b_router': nrm(ks[27], (DEPTH, N_EXPERTS), 0.01),
        'w_gate_up': nrm(ks[28], (DEPTH, N_EXPERTS, D_MODEL, 2 * D_FF), D_MODEL ** -0.5),
        'b_gate_up': nrm(ks[29], (DEPTH, N_EXPERTS, 2 * D_FF), 0.02),
        'w_down': nrm(ks[30], (DEPTH, N_EXPERTS, D_FF, D_MODEL), D_FF ** -0.5),
        'b_down': nrm(ks[31], (DEPTH, N_EXPERTS, D_MODEL), 0.02),
    }


def reference(x_prompt, x_sample, mem_prompt, cache_k, cache_v, cache_mem_k, cache_mem_v, state_pool, page_table,
              g_mix, w_in, g_q, g_k, lam, g_sub, w_pool, pool_scale, w_out,
              g_xattn, g_mem, w_xq, w_xkv, g_xq, g_xk, w_xo,
              g_ffn, w_router, b_router, w_gate_up, b_gate_up, w_down, b_down):
    b_p, s_p, _ = x_prompt.shape
    b_s, t_s, _ = x_sample.shape
    n_pages = page_table.shape[1]
    past = n_pages * cache_k.shape[2]
    slopes = alibi_slopes()
    hp, hs = x_prompt, x_sample
    kp_l, vp_l, poolp_l, mkp_l, mvp_l, ks_l, vs_l, pools_l = [], [], [], [], [], [], [], []
    for l in range(DEPTH):
        lam_init = 0.8 - 0.6 * math.exp(-0.3 * l)
        lam_l = diff_lambda(lam[l], lam_init)

        q, k, v, up = project_mixer(hp, g_mix[l], w_in[l], g_q[l], g_k[l])
        k_pos = jnp.arange(s_p)
        g_sub_l = g_sub[l]

        def q_block(i, q=q, k=k, v=v, lam_l=lam_l, lam_init=lam_init, g_sub_l=g_sub_l):
            qb = lax.dynamic_slice_in_dim(q, i * Q_BLOCK, Q_BLOCK, axis=1)
            q_pos = i * Q_BLOCK + jnp.arange(Q_BLOCK)
            return diff_attend(qb, k, v, q_pos, k_pos, slopes, lam_l, lam_init, g_sub_l)

        o_att = lax.map(q_block, jnp.arange(s_p // Q_BLOCK))
        o_att = jnp.moveaxis(o_att, 0, 1).reshape(b_p, s_p, ATT_WIDTH)
        o_pool = pool_mix(up, 0, w_pool[l], pool_scale[l])
        hp = hp + jnp.concatenate([o_att, o_pool], axis=-1) @ w_out[l]
        mk, mv = mem_kv(mem_prompt, g_mem[l], w_xkv[l], g_xk[l])
        hp = cross_attend(hp, g_xattn[l], mk, mv, w_xq[l], g_xq[l], w_xo[l])
        hp = moe_ffn(hp, g_ffn[l], w_router[l], b_router[l], w_gate_up[l], b_gate_up[l], w_down[l], b_down[l])
        kp_l.append(k.reshape(b_p, s_p, N_HEADS, V_DIM))
        vp_l.append(v)
        poolp_l.append(up[:, s_p - POOL_BUF:])
        mkp_l.append(mk)
        mvp_l.append(mv)

        qs, ks_new, vs_new, ups = project_mixer(hs, g_mix[l], w_in[l], g_q[l], g_k[l])
        past_k = cache_k[l][page_table].reshape(b_s, past, N_HEADS, 2, HEAD_DIM)
        past_v = cache_v[l][page_table].reshape(b_s, past, N_HEADS, V_DIM)
        k_all = jnp.concatenate([past_k, ks_new], axis=1)
        v_all = jnp.concatenate([past_v, vs_new], axis=1)
        o_att_s = diff_attend(qs, k_all, v_all, past + jnp.arange(t_s), jnp.arange(past + t_s),
                              slopes, lam_l, lam_init, g_sub_l).reshape(b_s, t_s, ATT_WIDTH)
        u_ext = jnp.concatenate([state_pool[l], ups], axis=1)
        o_pool_s = pool_mix(u_ext, POOL_BUF, w_pool[l], pool_scale[l])
        hs = hs + jnp.concatenate([o_att_s, o_pool_s], axis=-1) @ w_out[l]
        hs = cross_attend(hs, g_xattn[l], cache_mem_k[l], cache_mem_v[l], w_xq[l], g_xq[l], w_xo[l])
        hs = moe_ffn(hs, g_ffn[l], w_router[l], b_router[l], w_gate_up[l], b_gate_up[l], w_down[l], b_down[l])
        ks_l.append(ks_new.reshape(b_s, t_s, N_HEADS, V_DIM))
        vs_l.append(vs_new)
        pools_l.append(u_ext[:, u_ext.shape[1] - POOL_BUF:])

    new_k_prompt = jnp.stack(kp_l)
    new_v_prompt = jnp.stack(vp_l)
    new_pool_prompt = jnp.stack(poolp_l)
    new_mem_k_prompt = jnp.stack(mkp_l)
    new_mem_v_prompt = jnp.stack(mvp_l)
    new_k_sample = jnp.stack(ks_l)
    new_v_sample = jnp.stack(vs_l)
    new_pool_sample = jnp.stack(pools_l)
    return (hp, hs, new_k_prompt, new_v_prompt, new_pool_prompt, new_mem_k_prompt, new_mem_v_prompt,
            new_k_sample, new_v_sample, new_pool_sample)
```

```python
import functools
import math

import jax
import jax.numpy as jnp
from jax import lax
from jax.experimental import pallas as pl
from jax.experimental.pallas import tpu as pltpu

F32 = jnp.float32
BF16 = jnp.bfloat16
I32 = jnp.int32

EPS = 1e-6
N_HEADS = 4
HEAD_DIM = 64
V_DIM = 2 * HEAD_DIM
ATT_WIDTH = N_HEADS * V_DIM
POOL_WINDOWS = (2, 4, 8, 16)
POOL_BUF = max(POOL_WINDOWS) - 1
HALO = POOL_BUF + 1
X_HEADS = 4
X_HEAD_DIM = 64
TOP_K = 4
SWIGLU_LIMIT = 7.0
SWIGLU_ALPHA = 1.702
NEG = -1e30

LANES = 128
SUBLANES = 8
MXU_DIM = 256
VMEM_LIMIT = 56 * 1024 * 1024


def _params(*sem):
    return pltpu.CompilerParams(dimension_semantics=sem, vmem_limit_bytes=VMEM_LIMIT)


def _tile(n, pref):
    return pref if n % pref == 0 else n


def _rms(x, g):
    return x * lax.rsqrt(jnp.mean(x * x, axis=-1, keepdims=True) + EPS) * g


def _dot(a, b):
    return jnp.dot(a, b, preferred_element_type=F32)


def _dot_nt(a, b):
    return lax.dot_general(a, b, (((1,), (1,)), ((), ())), preferred_element_type=F32)


def _dot_tn(a, b):
    return lax.dot_general(a, b, (((0,), (0,)), ((), ())), preferred_element_type=F32)


def _group_meansq(u, gmat):
    sq = u * u
    hi = sq.astype(BF16)
    lo = (sq - hi.astype(F32)).astype(BF16)
    outs = []
    for c in range(u.shape[1] // MXU_DIM):
        sl = slice(c * MXU_DIM, (c + 1) * MXU_DIM)
        outs.append(_dot(hi[:, sl], gmat) + _dot(lo[:, sl], gmat))
    return jnp.concatenate(outs, axis=1) * (1.0 / HEAD_DIM)


def _proj_kernel(x_ref, gmix_ref, w_ref, gqk_ref, gmat_ref, qkv_ref, k_ref, v_ref, up_ref):
    aw = ATT_WIDTH
    xn = _rms(x_ref[...], gmix_ref[...])
    u = _dot(xn.astype(BF16), w_ref[...])
    qk = u[:, :2 * aw]
    qkn = qk * lax.rsqrt(_group_meansq(qk, gmat_ref[...]) + EPS) * gqk_ref[...]
    kn = qkn[:, aw:]
    v = u[:, 2 * aw:3 * aw]
    k_ref[...] = kn
    v_ref[...] = v
    up_ref[...] = u[:, 3 * aw:]
    qkv_ref[:, :aw] = (qkn[:, :aw] * (HEAD_DIM ** -0.5)).astype(BF16)
    qkv_ref[:, aw:2 * aw] = kn.astype(BF16)
    qkv_ref[:, 2 * aw:] = v.astype(BF16)


def _proj(x2d, g_mix, w_in_b, gqk, gmat):
    n, d = x2d.shape
    wtot = w_in_b.shape[1]
    aw = ATT_WIDTH
    tm = _tile(n, 512)
    full = lambda shape: pl.BlockSpec(shape, lambda i: (0,) * len(shape))
    return pl.pallas_call(
        _proj_kernel,
        grid=(n // tm,),
        in_specs=[pl.BlockSpec((tm, d), lambda i: (i, 0)), full((1, d)), full((d, wtot)),
                  full((1, 2 * aw)), full((MXU_DIM, MXU_DIM))],
        out_specs=[pl.BlockSpec((tm, 3 * aw), lambda i: (i, 0)),
                   pl.BlockSpec((tm, aw), lambda i: (i, 0)),
                   pl.BlockSpec((tm, aw), lambda i: (i, 0)),
                   pl.BlockSpec((tm, wtot - 3 * aw), lambda i: (i, 0))],
        out_shape=[jax.ShapeDtypeStruct((n, 3 * aw), BF16),
                   jax.ShapeDtypeStruct((n, aw), F32),
                   jax.ShapeDtypeStruct((n, aw), F32),
                   jax.ShapeDtypeStruct((n, wtot - 3 * aw), F32)],
        compiler_params=_params("parallel"),
        name="in_proj",
    )(x2d, g_mix, w_in_b, gqk, gmat)


def _lambda_value(lam_ref, lam_init):
    lf = lam_ref[...]
    a = jnp.sum(lf[0:1] * lf[1:2], axis=-1, keepdims=True)
    b = jnp.sum(lf[2:3] * lf[3:4], axis=-1, keepdims=True)
    return jnp.exp(a) - jnp.exp(b) + lam_init


def _sub_norm(od, gsub, lam_init):
    return _rms(od, gsub) * (1.0 - lam_init)


def _softmax_step(s, vb, m, l, acc):
    m_new = jnp.maximum(m, jnp.max(s, axis=-1, keepdims=True))
    alpha = jnp.exp(m - m_new)
    p = jnp.exp(s - m_new)
    l = alpha * l + jnp.sum(p, axis=-1, keepdims=True)
    acc = alpha * acc + _dot(p.astype(BF16), vb)
    return m_new, l, acc


def _attn_kernel(slopes_ref, lam_ref, gsub_ref, q_ref, k_ref, v_ref, o_ref, *, tq, tk, lam_init):
    h = pl.program_id(1)
    i = pl.program_id(2)
    slope = slopes_ref[h]
    q = q_ref[0]
    lane = lax.broadcasted_iota(I32, q.shape, 1)
    zero = jnp.zeros_like(q)
    qz = jnp.concatenate([jnp.where(lane < HEAD_DIM, q, zero), jnp.where(lane >= HEAD_DIM, q, zero)], axis=0)
    rows = lax.broadcasted_iota(I32, (2 * tq, tk), 0)
    rows = jnp.where(rows >= tq, rows - tq, rows)
    cols = lax.broadcasted_iota(I32, (2 * tq, tk), 1)
    rel = rows - cols + i * tq
    n_kv = ((i + 1) * tq + tk - 1) // tk

    def body(j, carry):
        m, l, acc = carry
        k0 = pl.multiple_of(j * tk, tk)
        kb = k_ref[0, pl.ds(k0, tk), :]
        vb = v_ref[0, pl.ds(k0, tk), :]
        dist = rel - j * tk
        s = _dot_nt(qz, kb) - slope * dist.astype(F32)
        s = jnp.where(dist >= 0, s, NEG)
        return _softmax_step(s, vb, m, l, acc)

    m0 = jnp.full((2 * tq, 1), NEG, F32)
    l0 = jnp.zeros((2 * tq, 1), F32)
    a0 = jnp.zeros((2 * tq, V_DIM), F32)
    _, l, acc = lax.fori_loop(0, n_kv, body, (m0, l0, a0))
    o = acc / l
    od = o[:tq] - _lambda_value(lam_ref, lam_init) * o[tq:]
    o_ref[0] = _sub_norm(od, gsub_ref[...], lam_init).astype(o_ref.dtype)


def _attn(qkv, slopes, lam, gsub, lam_init):
    b, s, _ = qkv.shape
    tq = _tile(s, 256)
    tk = _tile(s, 512)
    nh = N_HEADS
    kern = functools.partial(_attn_kernel, tq=tq, tk=tk, lam_init=lam_init)
    return pl.pallas_call(
        kern,
        grid=(b, nh, s // tq),
        in_specs=[pl.BlockSpec(memory_space=pltpu.SMEM),
                  pl.BlockSpec(lam.shape, lambda bi, h, i: (0, 0)),
                  pl.BlockSpec((1, V_DIM), lambda bi, h, i: (0, 0)),
                  pl.BlockSpec((1, tq, V_DIM), lambda bi, h, i: (bi, i, h)),
                  pl.BlockSpec((1, s, V_DIM), lambda bi, h, i: (bi, 0, nh + h)),
                  pl.BlockSpec((1, s, V_DIM), lambda bi, h, i: (bi, 0, 2 * nh + h))],
        out_specs=pl.BlockSpec((1, tq, V_DIM), lambda bi, h, i: (bi, i, h)),
        out_shape=jax.ShapeDtypeStruct((b, s, ATT_WIDTH), BF16),
        compiler_params=_params("parallel", "parallel", "arbitrary"),
        name="prompt_attn",
    )(slopes, lam, gsub, qkv, qkv, qkv)


def _page_copies(pt_ref, ck_hbm, cv_hbm, kbuf, vbuf, sem, step, slot, *, n_chunks, pages_per_chunk, page):
    b = step // n_chunks
    c = step % n_chunks
    n_pages = n_chunks * pages_per_chunk
    copies = []
    for p in range(pages_per_chunk):
        phys = pt_ref[b * n_pages + c * pages_per_chunk + p]
        dst = pl.ds(p * page, page)
        copies.append(pltpu.make_async_copy(ck_hbm.at[phys], kbuf.at[slot, dst], sem.at[0, slot]))
        copies.append(pltpu.make_async_copy(cv_hbm.at[phys], vbuf.at[slot, dst], sem.at[1, slot]))
    return copies


def _sample_attn_kernel(pt_ref, slopes_ref, lam_ref, gsub_ref, q_ref, kn_ref, vn_ref, ck_hbm, cv_hbm, o_ref,
                        kbuf, vbuf, sem, m_ref, l_ref, acc_ref, *, n_chunks, pages_per_chunk, page, t_new,
                        lam_init):
    b = pl.program_id(0)
    c = pl.program_id(1)
    step = b * n_chunks + c
    n_steps = pl.num_programs(0) * n_chunks
    slot = step % 2
    copies = functools.partial(_page_copies, pt_ref, ck_hbm, cv_hbm, kbuf, vbuf, sem,
                               n_chunks=n_chunks, pages_per_chunk=pages_per_chunk, page=page)

    @pl.when(step == 0)
    def _():
        for cp in copies(step, slot):
            cp.start()

    @pl.when(step + 1 < n_steps)
    def _():
        for cp in copies(step + 1, 1 - slot):
            cp.start()

    @pl.when(c == 0)
    def _():
        m_ref[...] = jnp.full(m_ref.shape, NEG, F32)
        l_ref[...] = jnp.zeros(l_ref.shape, F32)
        acc_ref[...] = jnp.zeros(acc_ref.shape, F32)

    qz = q_ref[0]
    n_rows = qz.shape[0]
    ck = pages_per_chunk * page
    past = n_chunks * ck
    rows = lax.broadcasted_iota(I32, (n_rows, 1), 0)
    tok = rows % t_new
    head = rows // (2 * t_new)
    slope = jnp.zeros((n_rows, 1), F32)
    for hh in range(N_HEADS):
        slope = jnp.where(head == hh, slopes_ref[hh], slope)

    for cp in copies(step, slot):
        cp.wait()

    kb = kbuf[slot].astype(BF16)
    vb = vbuf[slot].astype(BF16)
    cols = lax.broadcasted_iota(I32, (n_rows, ck), 1)
    dist = (past + tok) - (c * ck + cols)
    s = _dot_nt(qz, kb) - slope * dist.astype(F32)
    m, l, acc = _softmax_step(s, vb, m_ref[...], l_ref[...], acc_ref[...])
    m_ref[...] = m
    l_ref[...] = l
    acc_ref[...] = acc

    @pl.when(c == n_chunks - 1)
    def _():
        knb = kn_ref[0].astype(BF16)
        vnb = vn_ref[0].astype(BF16)
        ncol = lax.broadcasted_iota(I32, (n_rows, knb.shape[0]), 1)
        nd = tok - ncol
        sn = _dot_nt(qz, knb) - slope * nd.astype(F32)
        sn = jnp.where((nd >= 0) & (ncol < t_new), sn, NEG)
        _, l2, acc2 = _softmax_step(sn, vnb, m, l, acc)
        o = acc2 / l2
        lanes = lax.broadcasted_iota(I32, o.shape, 1)
        o = jnp.where(lanes // V_DIM == head, o, 0.0)
        r = o.reshape(N_HEADS, 2 * t_new, o.shape[1]).sum(axis=0)
        od = r - _lambda_value(lam_ref, lam_init) * pltpu.roll(r, t_new, axis=0)
        outs = [_sub_norm(od[:, hh * V_DIM:(hh + 1) * V_DIM], gsub_ref[...], lam_init) for hh in range(N_HEADS)]
        o_ref[0] = jnp.concatenate(outs, axis=1).astype(o_ref.dtype)


def _sample_attn(q, kn, vn, cache_k, cache_v, page_table, slopes, lam, gsub, lam_init):
    b, t_new, width = q.shape
    n_pages = page_table.shape[1]
    page = cache_k.shape[1]
    ppc = 8 if n_pages % 8 == 0 else n_pages
    n_chunks = n_pages // ppc
    n_rows = N_HEADS * 2 * t_new
    assert 2 * t_new == SUBLANES, "row regrouping assumes two maps of four tokens fill one sublane tile"
    lane_grp = (jnp.arange(width) // HEAD_DIM)[None, :]
    row_grp = jnp.arange(2 * N_HEADS)[:, None]
    mask = (lane_grp == row_grp).astype(q.dtype)
    qz = (q[:, None, :, :] * mask[None, :, None, :]).reshape(b, n_rows, width)
    pad = lambda a: jnp.pad(a, ((0, 0), (0, SUBLANES - t_new), (0, 0)))
    kern = functools.partial(_sample_attn_kernel, n_chunks=n_chunks, pages_per_chunk=ppc, page=page,
                             t_new=t_new, lam_init=lam_init)
    grid_spec = pltpu.PrefetchScalarGridSpec(
        num_scalar_prefetch=1,
        grid=(b, n_chunks),
        in_specs=[pl.BlockSpec(memory_space=pltpu.SMEM),
                  pl.BlockSpec(lam.shape, lambda bi, c, pt: (0, 0)),
                  pl.BlockSpec((1, V_DIM), lambda bi, c, pt: (0, 0)),
                  pl.BlockSpec((1, n_rows, width), lambda bi, c, pt: (bi, 0, 0)),
                  pl.BlockSpec((1, SUBLANES, width), lambda bi, c, pt: (bi, 0, 0)),
                  pl.BlockSpec((1, SUBLANES, width), lambda bi, c, pt: (bi, 0, 0)),
                  pl.BlockSpec(memory_space=pl.ANY),
                  pl.BlockSpec(memory_space=pl.ANY)],
        out_specs=pl.BlockSpec((1, SUBLANES, width), lambda bi, c, pt: (bi, 0, 0)),
        scratch_shapes=[pltpu.VMEM((2, ppc * page, width), F32),
                        pltpu.VMEM((2, ppc * page, width), F32),
                        pltpu.SemaphoreType.DMA((2, 2)),
                        pltpu.VMEM((n_rows, 1), F32),
                        pltpu.VMEM((n_rows, 1), F32),
                        pltpu.VMEM((n_rows, width), F32)])
    out = pl.pallas_call(
        kern,
        grid_spec=grid_spec,
        out_shape=jax.ShapeDtypeStruct((b, SUBLANES, width), BF16),
        compiler_params=_params("arbitrary", "arbitrary"),
        name="sample_attn",
    )(page_table.reshape(-1), slopes, lam, gsub, qz, pad(kn), pad(vn), cache_k, cache_v)
    return out[:, :t_new]


def _outproj_kernel(x_ref, oatt_ref, up_ref, halo_ref, wbd_ref, pscale_ref, wout_ref, o_ref, ext_ref, *,
                    tm, n_prev, zero_first_halo):
    i = pl.program_id(1)
    halo = halo_ref[0]
    if zero_first_halo:
        halo = jnp.where(i == 0, 0.0, halo)
    cur = up_ref[0]
    ext_ref[0:HALO, :] = halo
    ext_ref[HALO:HALO + tm, :] = cur
    gw = cur.shape[1] // len(POOL_WINDOWS)
    pos1 = lax.broadcasted_iota(I32, (tm, gw), 0) + (i * tm + n_prev + 1)
    ds = []
    for g, w in enumerate(POOL_WINDOWS):
        cs = slice(g * gw, (g + 1) * gw)
        acc = cur[:, cs]
        for sft in range(1, w):
            acc = acc + ext_ref[HALO - sft:HALO - sft + tm, cs]
        cnt = jnp.minimum(pos1, w).astype(F32)
        ds.append(acc / cnt - cur[:, cs])
    d = jnp.concatenate(ds, axis=1)
    o_pool = _dot(d.astype(BF16), wbd_ref[...]) * pscale_ref[...]
    aw = oatt_ref.shape[2]
    o_ref[0] = (x_ref[0] + _dot(oatt_ref[0], wout_ref[0:aw, :])
                + _dot(o_pool.astype(BF16), wout_ref[aw:, :]))


def _outproj(x, o_att, up, halo_arr, wbd, pool_scale, w_out_b, *, n_prev, halo_from_up):
    b, t, d = x.shape
    aw = o_att.shape[2]
    pw = up.shape[2]
    tm = _tile(t, 512)
    if halo_from_up:
        halo_map = lambda bi, i: (bi, jnp.maximum(i * (tm // HALO) - 1, 0), 0)
    else:
        halo_map = lambda bi, i: (bi, 0, 0)
    kern = functools.partial(_outproj_kernel, tm=tm, n_prev=n_prev, zero_first_halo=halo_from_up)
    full = lambda shape: pl.BlockSpec(shape, lambda bi, i: (0,) * len(shape))
    return pl.pallas_call(
        kern,
        grid=(b, t // tm),
        in_specs=[pl.BlockSpec((1, tm, d), lambda bi, i: (bi, i, 0)),
                  pl.BlockSpec((1, tm, aw), lambda bi, i: (bi, i, 0)),
                  pl.BlockSpec((1, tm, pw), lambda bi, i: (bi, i, 0)),
                  pl.BlockSpec((1, HALO, pw), halo_map),
                  full((pw, pw)), full((1, pw)), full((aw + pw, d))],
        out_specs=pl.BlockSpec((1, tm, d), lambda bi, i: (bi, i, 0)),
        out_shape=jax.ShapeDtypeStruct((b, t, d), F32),
        scratch_shapes=[pltpu.VMEM((HALO + tm, pw), F32)],
        compiler_params=_params("parallel", "arbitrary"),
        name="pool_out_proj",
    )(x, o_att, up, halo_arr, wbd, pool_scale, w_out_b)


def _memkv_kernel(mem_ref, gmem_ref, w_ref, gxk_ref, gmat_ref, mk_ref, mv_ref):
    xw = mk_ref.shape[1]
    kv = _dot(_rms(mem_ref[...], gmem_ref[...]).astype(BF16), w_ref[...])
    kx = kv[:, :xw]
    mk_ref[...] = kx * lax.rsqrt(_group_meansq(kx, gmat_ref[...]) + EPS) * gxk_ref[...]
    mv_ref[...] = kv[:, xw:]


def _memkv(mem2d, g_mem, w_xkv_b, gxk, gmat):
    n, d = mem2d.shape
    xw = w_xkv_b.shape[1] // 2
    tm = _tile(n, 512)
    full = lambda shape: pl.BlockSpec(shape, lambda i: (0,) * len(shape))
    return pl.pallas_call(
        _memkv_kernel,
        grid=(n // tm,),
        in_specs=[pl.BlockSpec((tm, d), lambda i: (i, 0)), full((1, d)), full((d, 2 * xw)),
                  full((1, xw)), full((MXU_DIM, MXU_DIM))],
        out_specs=[pl.BlockSpec((tm, xw), lambda i: (i, 0))] * 2,
        out_shape=[jax.ShapeDtypeStruct((n, xw), F32)] * 2,
        compiler_params=_params("parallel"),
        name="mem_kv",
    )(mem2d, g_mem, w_xkv_b, gxk, gmat)


def _xattn_kernel(h_ref, mk_ref, mv_ref, gx_ref, wq_ref, gxq_ref, gmat_ref, wo_ref, o_ref):
    h1 = h_ref[0]
    qx = _dot(_rms(h1, gx_ref[...]).astype(BF16), wq_ref[...])
    qn = (qx * lax.rsqrt(_group_meansq(qx, gmat_ref[...]) + EPS) * gxq_ref[...]).astype(BF16)
    mk = mk_ref[0].astype(BF16)
    mv = mv_ref[0].astype(BF16)
    head_of_lane = lax.broadcasted_iota(I32, qn.shape, 1) // X_HEAD_DIM
    o = jnp.zeros(qn.shape, F32)
    for hh in range(X_HEADS):
        sel = head_of_lane == hh
        s = _dot_nt(jnp.where(sel, qn, jnp.zeros_like(qn)), mk)
        p = jnp.exp(s - jnp.max(s, axis=-1, keepdims=True))
        p = p / jnp.sum(p, axis=-1, keepdims=True)
        o = o + jnp.where(sel, _dot(p.astype(BF16), mv), 0.0)
    o_ref[0] = h1 + _dot(o.astype(BF16), wo_ref[...])


def _xattn(h1, mk, mv, g_xattn, w_xq_b, gxq, gmat, w_xo_b):
    b, t, d = h1.shape
    nm, xw = mk.shape[1], mk.shape[2]
    tq = _tile(t, 512)
    full = lambda shape: pl.BlockSpec(shape, lambda bi, i: (0,) * len(shape))
    return pl.pallas_call(
        _xattn_kernel,
        grid=(b, t // tq),
        in_specs=[pl.BlockSpec((1, tq, d), lambda bi, i: (bi, i, 0)),
                  pl.BlockSpec((1, nm, xw), lambda bi, i: (bi, 0, 0)),
                  pl.BlockSpec((1, nm, xw), lambda bi, i: (bi, 0, 0)),
                  full((1, d)), full((d, xw)), full((1, xw)), full((MXU_DIM, MXU_DIM)), full((xw, d))],
        out_specs=pl.BlockSpec((1, tq, d), lambda bi, i: (bi, i, 0)),
        out_shape=jax.ShapeDtypeStruct((b, t, d), F32),
        compiler_params=_params("parallel", "parallel"),
        name="mem_xattn",
    )(h1, mk, mv, g_xattn, w_xq_b, gxq, gmat, w_xo_b)


def _gather_rows(t, n_experts):
    return -(-(TOP_K * t + n_experts * (SUBLANES - 1)) // LANES) * LANES


def _slot_onehot(pos, n_rows):
    t = pos.shape[0]
    r = lax.broadcasted_iota(I32, (t, n_rows), 1)
    hit = r == pos[:, 0:1]
    for k in range(1, TOP_K):
        hit = hit | (r == pos[:, k:k + 1])
    return jnp.where(hit, 1.0, 0.0).astype(BF16)


def _route_kernel(h_ref, gffn_ref, wr_ref, br_ref, xg_ref, pos_ref, cnt_ref, *, n_experts):
    t = h_ref.shape[0]
    n_rows = xg_ref.shape[1]
    mb = _rms(h_ref[...], gffn_ref[...]).astype(BF16)
    logits = _dot(mb, wr_ref[...]) + br_ref[...]
    lane = lax.broadcasted_iota(I32, logits.shape, 1).astype(F32)
    work = logits
    sels, vals = [], []
    for _ in range(TOP_K):
        mx = jnp.max(work, axis=-1, keepdims=True)
        idx = jnp.min(jnp.where(work == mx, lane, float(LANES)), axis=-1, keepdims=True)
        sel = lane == idx
        sels.append(sel)
        vals.append(mx)
        work = jnp.where(sel, 2.0 * NEG, work)
    es = [jnp.exp(v - vals[0]) for v in vals]
    denom = es[0]
    for e in es[1:]:
        denom = denom + e
    hot = jnp.zeros(logits.shape, F32)
    gate_m = jnp.zeros(logits.shape, F32)
    for sel, e in zip(sels, es):
        hot = jnp.where(sel, 1.0, hot)
        gate_m = jnp.where(sel, e / denom, gate_m)
    ri = lax.broadcasted_iota(I32, (t, t), 0)
    ci = lax.broadcasted_iota(I32, (t, t), 1)
    rank = _dot(jnp.where(ci < ri, 1.0, 0.0).astype(BF16), hot.astype(BF16))
    count = jnp.sum(hot, axis=0, keepdims=True).astype(I32)
    c8 = (((count + (SUBLANES - 1)) >> 3) << 3).astype(F32)
    ei = lax.broadcasted_iota(I32, (LANES, LANES), 0)
    ej = lax.broadcasted_iota(I32, (LANES, LANES), 1)
    off = _dot(jnp.broadcast_to(c8, (SUBLANES, LANES)).astype(BF16), jnp.where(ei < ej, 1.0, 0.0).astype(BF16))[0:1]
    slot = off + rank
    lane_i = lax.broadcasted_iota(I32, logits.shape, 1)
    pos = jnp.full(logits.shape, -1, I32)
    for k, sel in enumerate(sels):
        pk = jnp.sum(jnp.where(sel, slot, 0.0), axis=-1, keepdims=True).astype(I32)
        pos = jnp.where(lane_i == k, pk, pos)
    pos_ref[...] = pos
    cnt_ref[0] = jnp.broadcast_to(count, (SUBLANES, LANES))
    g1 = gate_m.astype(BF16)
    r1 = gate_m - g1.astype(F32)
    g2 = r1.astype(BF16)
    g3 = (r1 - g2.astype(F32)).astype(BF16)
    gx = (g1.astype(F32) + pltpu.roll(g2.astype(F32), n_experts, axis=1)
          + pltpu.roll(g3.astype(F32), 2 * n_experts, axis=1)).astype(BF16)
    xg_ref[0] = _dot_tn(_slot_onehot(pos, n_rows), jnp.concatenate([mb, gx], axis=1))


def _route(h2d, g_ffn, wr_pad_b, br_pad, n_experts):
    n, d = h2d.shape
    t = _tile(n, 256)
    n_sub = n // t
    n_rows = _gather_rows(t, n_experts)
    kern = functools.partial(_route_kernel, n_experts=n_experts)
    full = lambda shape: pl.BlockSpec(shape, lambda i: (0,) * len(shape))
    return pl.pallas_call(
        kern,
        grid=(n_sub,),
        in_specs=[pl.BlockSpec((t, d), lambda i: (i, 0)), full((1, d)), full((d, LANES)), full((1, LANES))],
        out_specs=[pl.BlockSpec((1, n_rows, d + LANES), lambda i: (i, 0, 0)),
                   pl.BlockSpec((t, LANES), lambda i: (i, 0)),
                   pl.BlockSpec((1, SUBLANES, LANES), lambda i: (i, 0, 0))],
        out_shape=[jax.ShapeDtypeStruct((n_sub, n_rows, d + LANES), F32),
                   jax.ShapeDtypeStruct((n, LANES), I32),
                   jax.ShapeDtypeStruct((n_sub, SUBLANES, LANES), I32)],
        compiler_params=_params("parallel"),
        name="moe_route_gather",
    )(h2d, g_ffn, wr_pad_b, br_pad)


def _unit_plan(cnt, n_rows, n_experts):
    n_sub = cnt.shape[0]
    upt = n_rows // SUBLANES
    u = (cnt + (SUBLANES - 1)) // SUBLANES
    uoff = jnp.cumsum(u, axis=1) - u
    used = jnp.sum(u, axis=1, keepdims=True)
    u_all = jnp.concatenate([u, upt - used], axis=1)
    uoff_all = jnp.concatenate([uoff, used], axis=1)
    flat_cnt = u_all.T.reshape(-1)
    cum = jnp.cumsum(flat_cnt)
    start = cum - flat_cnt
    j = jnp.arange(n_sub * upt, dtype=I32)
    cum2d = cum.reshape(n_experts + 1, n_sub)
    e_of = jnp.sum(cum2d[None, :, -1] <= j[:, None], axis=1)
    tile_of = jnp.sum(cum2d[e_of] <= j[:, None], axis=1)
    chunk = (e_of * n_sub + tile_of).astype(I32)
    base = tile_of * upt + uoff_all.T.reshape(-1)[chunk]
    unit_src = (base + (j - start[chunk])).astype(I32)
    ustart = jnp.concatenate([start[::n_sub], cum[-1:]]).astype(I32)
    return ustart, unit_src


def _expert_kernel(ustart_ref, usrc_ref, xg_hbm, wgu_ref, bgu_ref, wd_ref, bd_ref, yg_hbm,
                   xbuf, ybuf, wgu_b, wd_b, in_sem, out_sem, *, n_experts, upb, d_model):
    e = pl.program_id(0)
    u0 = ustart_ref[e]
    u1 = ustart_ref[e + 1]
    dff = wd_b.shape[0]

    def in_copy(j, slot, i):
        return pltpu.make_async_copy(xg_hbm.at[usrc_ref[j]], xbuf.at[slot, pl.ds(i * SUBLANES, SUBLANES)],
                                     in_sem.at[slot])

    def out_copy(j, i):
        return pltpu.make_async_copy(ybuf.at[pl.ds(i * SUBLANES, SUBLANES)], yg_hbm.at[usrc_ref[j]], out_sem.at[0])

    def n_valid(blk):
        return jnp.minimum(upb, u1 - (u0 + blk * upb))

    def for_units(blk, fn):
        base = u0 + blk * upb

        def body(i, c):
            fn(base + i, i)
            return c
        lax.fori_loop(0, n_valid(blk), body, 0)

    @pl.when(e == 0)
    def _():
        xbuf[...] = jnp.zeros(xbuf.shape, F32)

    @pl.when(e < n_experts)
    def _():
        rc = 128
        for r in range(0, wgu_b.shape[0], rc):
            wgu_b[r:r + rc, :] = wgu_ref[0, r:r + rc, :].astype(BF16)
        for r in range(0, wd_b.shape[0], rc):
            wd_b[r:r + rc, :] = wd_ref[0, r:r + rc, :].astype(BF16)
        n_blk = (u1 - u0 + upb - 1) // upb

        @pl.when(n_blk > 0)
        def _():
            for_units(0, lambda j, i: in_copy(j, 0, i).start())

        def block(blk, c):
            slot = blk % 2

            @pl.when(blk + 1 < n_blk)
            def _():
                for_units(blk + 1, lambda j, i: in_copy(j, 1 - slot, i).start())

            for_units(blk, lambda j, i: in_copy(j, slot, i).wait())
            x = xbuf[slot]
            gcols = x[:, d_model:]
            lane = lax.broadcasted_iota(I32, gcols.shape, 1)
            own = (lane % n_experts == e) & (lane < 3 * n_experts)
            gate_w = jnp.sum(jnp.where(own, gcols, 0.0), axis=-1, keepdims=True)
            gu = _dot(x[:, :d_model].astype(BF16), wgu_b[...]) + bgu_ref[0]
            gate = jnp.minimum(gu[:, :dff], SWIGLU_LIMIT)
            up = jnp.clip(gu[:, dff:], -SWIGLU_LIMIT, SWIGLU_LIMIT)
            glu = gate * (1.0 / (1.0 + jnp.exp(-SWIGLU_ALPHA * gate)))
            y = (_dot(((up + 1.0) * glu).astype(BF16), wd_b[...]) + bd_ref[0]) * gate_w

            @pl.when(blk > 0)
            def _():
                for_units(blk - 1, lambda j, i: out_copy(j, i).wait())

            ybuf[...] = y
            for_units(blk, lambda j, i: out_copy(j, i).start())
            return c

        lax.fori_loop(0, n_blk, block, 0)

        @pl.when(n_blk > 0)
        def _():
            for_units(n_blk - 1, lambda j, i: out_copy(j, i).wait())

    @pl.when(e == n_experts)
    def _():
        ybuf[...] = jnp.zeros(ybuf.shape, F32)

        def start(j, c):
            out_copy(j, 0).start()
            return c

        def wait(j, c):
            out_copy(j, 0).wait()
            return c
        lax.fori_loop(u0, u1, start, 0)
        lax.fori_loop(u0, u1, wait, 0)


def _experts(ustart, unit_src, xg, w_gu, b_gu, w_d, b_d, tr):
    n_sub, n_rows, dx = xg.shape
    n_experts, d_model, dff2 = w_gu.shape
    dff = w_d.shape[1]
    n_units = n_sub * n_rows // SUBLANES
    upb = tr // SUBLANES
    kern = functools.partial(_expert_kernel, n_experts=n_experts, upb=upb, d_model=d_model)
    last = n_experts - 1
    grid_spec = pltpu.PrefetchScalarGridSpec(
        num_scalar_prefetch=2,
        grid=(n_experts + 1,),
        in_specs=[pl.BlockSpec(memory_space=pl.ANY),
                  pl.BlockSpec((1, d_model, dff2), lambda e, us, src: (jnp.minimum(e, last), 0, 0)),
                  pl.BlockSpec((1, 1, dff2), lambda e, us, src: (jnp.minimum(e, last), 0, 0)),
                  pl.BlockSpec((1, dff, d_model), lambda e, us, src: (jnp.minimum(e, last), 0, 0)),
                  pl.BlockSpec((1, 1, d_model), lambda e, us, src: (jnp.minimum(e, last), 0, 0))],
        out_specs=pl.BlockSpec(memory_space=pl.ANY),
        scratch_shapes=[pltpu.VMEM((2, tr, dx), F32),
                        pltpu.VMEM((tr, d_model), F32),
                        pltpu.VMEM((d_model, dff2), BF16),
                        pltpu.VMEM((dff, d_model), BF16),
                        pltpu.SemaphoreType.DMA((2,)),
                        pltpu.SemaphoreType.DMA((1,))])
    yg = pl.pallas_call(
        kern,
        grid_spec=grid_spec,
        out_shape=jax.ShapeDtypeStruct((n_units, SUBLANES, d_model), F32),
        compiler_params=_params("arbitrary"),
        name="moe_experts",
    )(ustart, unit_src, xg.reshape(n_units, SUBLANES, dx), w_gu, b_gu.reshape(n_experts, 1, dff2),
      w_d, b_d.reshape(n_experts, 1, d_model))
    return yg.reshape(n_sub, n_rows, d_model)


def _combine_kernel(yg_ref, pos_ref, h_ref, o_ref):
    onehot = _slot_onehot(pos_ref[...], yg_ref.shape[1])
    o_ref[...] = h_ref[...] + _dot(onehot, yg_ref[0].astype(BF16))


def _combine(yg, pos, h2d):
    n_sub, n_rows, d = yg.shape
    n = h2d.shape[0]
    t = n // n_sub
    return pl.pallas_call(
        _combine_kernel,
        grid=(n_sub,),
        in_specs=[pl.BlockSpec((1, n_rows, d), lambda i: (i, 0, 0)),
                  pl.BlockSpec((t, LANES), lambda i: (i, 0)),
                  pl.BlockSpec((t, d), lambda i: (i, 0))],
        out_specs=pl.BlockSpec((t, d), lambda i: (i, 0)),
        out_shape=jax.ShapeDtypeStruct((n, d), F32),
        compiler_params=_params("parallel"),
        name="moe_combine",
    )(yg, pos, h2d)


def _moe(h, g_ffn, wr_pad_b, br_pad, w_gu, b_gu, w_d, b_d):
    b, t, d = h.shape
    n_experts = w_gu.shape[0]
    h2d = h.reshape(b * t, d)
    xg, pos, cnt = _route(h2d, g_ffn, wr_pad_b, br_pad, n_experts)
    ustart, unit_src = _unit_plan(cnt[:, 0, :n_experts], xg.shape[1], n_experts)
    tr = 256 if b * t >= 4096 else 128
    yg = _experts(ustart, unit_src, xg, w_gu, b_gu, w_d, b_d, tr)
    return _combine(yg, pos, h2d).reshape(b, t, d)


def kernel(x_prompt, x_sample, mem_prompt, cache_k, cache_v, cache_mem_k, cache_mem_v, state_pool, page_table,
           g_mix, w_in, g_q, g_k, lam, g_sub, w_pool, pool_scale, w_out,
           g_xattn, g_mem, w_xq, w_xkv, g_xq, g_xk, w_xo,
           g_ffn, w_router, b_router, w_gate_up, b_gate_up, w_down, b_down):
    depth = w_in.shape[0]
    b_p, s_p, d = x_prompt.shape
    b_s, t_s, _ = x_sample.shape
    n_mem = mem_prompt.shape[1]
    n_experts = w_router.shape[2]
    page = cache_k.shape[2]
    assert LANES % n_experts == 0 and 3 * n_experts <= LANES
    assert ATT_WIDTH % MXU_DIM == 0 and (X_HEADS * X_HEAD_DIM) % MXU_DIM == 0

    slopes = jnp.asarray([2.0 ** (-8.0 * (h + 1) / N_HEADS) for h in range(N_HEADS)], F32)
    grp = jnp.arange(MXU_DIM) // HEAD_DIM
    gmat = (grp[:, None] == grp[None, :]).astype(BF16)
    row = lambda a: a.reshape(1, -1).astype(F32)

    hp, hs = x_prompt, x_sample
    outs = [[] for _ in range(8)]
    for l in range(depth):
        lam_init = 0.8 - 0.6 * math.exp(-0.3 * l)
        w_in_b = w_in[l].astype(BF16)
        w_out_b = w_out[l].astype(BF16)
        w_xq_b = w_xq[l].astype(BF16)
        w_xkv_b = w_xkv[l].astype(BF16)
        w_xo_b = w_xo[l].astype(BF16)
        gqk = jnp.concatenate([jnp.tile(g_q[l].reshape(-1), N_HEADS), jnp.tile(g_k[l].reshape(-1), N_HEADS)])[None, :]
        gxq = (jnp.tile(g_xq[l], X_HEADS) * (X_HEAD_DIM ** -0.5))[None, :]
        gxk = jnp.tile(g_xk[l], X_HEADS)[None, :]
        wbd = jax.scipy.linalg.block_diag(*[w_pool[l, g] for g in range(len(POOL_WINDOWS))]).astype(BF16)
        wr_pad_b = jnp.pad(w_router[l], ((0, 0), (0, LANES - n_experts))).astype(BF16)
        br_pad = jnp.pad(b_router[l].astype(F32), (0, LANES - n_experts), constant_values=NEG)[None, :]
        moe_w = (row(g_ffn[l]), wr_pad_b, br_pad, w_gate_up[l], b_gate_up[l], w_down[l], b_down[l])
        gsub = row(g_sub[l])

        qkv, k32, v32, up = _proj(hp.reshape(b_p * s_p, d), row(g_mix[l]), w_in_b, gqk, gmat)
        o_att = _attn(qkv.reshape(b_p, s_p, -1), slopes, lam[l], gsub, lam_init)
        up3 = up.reshape(b_p, s_p, -1)
        h1 = _outproj(hp, o_att, up3, up3, wbd, row(pool_scale[l]), w_out_b, n_prev=0, halo_from_up=True)
        mk, mv = _memkv(mem_prompt.reshape(b_p * n_mem, d), row(g_mem[l]), w_xkv_b, gxk, gmat)
        h2 = _xattn(h1, mk.reshape(b_p, n_mem, -1), mv.reshape(b_p, n_mem, -1), row(g_xattn[l]), w_xq_b, gxq, gmat,
                    w_xo_b)
        hp = _moe(h2, *moe_w)
        outs[0].append(k32.reshape(b_p, s_p, N_HEADS, V_DIM))
        outs[1].append(v32.reshape(b_p, s_p, N_HEADS, V_DIM))
        outs[2].append(up3[:, s_p - POOL_BUF:])
        outs[3].append(mk.reshape(b_p, n_mem, X_HEADS, X_HEAD_DIM))
        outs[4].append(mv.reshape(b_p, n_mem, X_HEADS, X_HEAD_DIM))

        qkv_s, k32_s, v32_s, up_s = _proj(hs.reshape(b_s * t_s, d), row(g_mix[l]), w_in_b, gqk, gmat)
        q_s = qkv_s[:, :ATT_WIDTH].reshape(b_s, t_s, -1)
        kn = k32_s.reshape(b_s, t_s, -1)
        vn = v32_s.reshape(b_s, t_s, -1)
        n_phys = cache_k.shape[1]
        o_att_s = _sample_attn(q_s, kn, vn, cache_k[l].reshape(n_phys, page, -1), cache_v[l].reshape(n_phys, page, -1),
                               page_table, slopes, lam[l], gsub, lam_init)
        ups = up_s.reshape(b_s, t_s, -1)
        halo_s = jnp.pad(state_pool[l], ((0, 0), (HALO - POOL_BUF, 0), (0, 0)))
        h1s = _outproj(hs, o_att_s, ups, halo_s, wbd, row(pool_scale[l]), w_out_b, n_prev=POOL_BUF, halo_from_up=False)
        h2s = _xattn(h1s, cache_mem_k[l].reshape(b_s, n_mem, -1), cache_mem_v[l].reshape(b_s, n_mem, -1),
                     row(g_xattn[l]), w_xq_b, gxq, gmat, w_xo_b)
        hs = _moe(h2s, *moe_w)
        outs[5].append(kn.reshape(b_s, t_s, N_HEADS, V_DIM))
        outs[6].append(vn.reshape(b_s, t_s, N_HEADS, V_DIM))
        outs[7].append(jnp.concatenate([state_pool[l], ups], axis=1)[:, t_s:])

    stacked = [jnp.stack(o) for o in outs]
    return (hp, hs, *stacked)
```

```python
import functools
import math

import jax
import jax.numpy as jnp
from jax import lax
from jax.experimental import pallas as pl
from jax.experimental.pallas import tpu as pltpu

F32 = jnp.float32
BF16 = jnp.bfloat16
I32 = jnp.int32

EPS = 1e-6
N_HEADS = 4
HEAD_DIM = 64
V_DIM = 2 * HEAD_DIM
ATT_WIDTH = N_HEADS * V_DIM
POOL_WINDOWS = (2, 4, 8, 16)
POOL_BUF = max(POOL_WINDOWS) - 1
HALO = POOL_BUF + 1
X_HEADS = 4
X_HEAD_DIM = 64
TOP_K = 4
SWIGLU_LIMIT = 7.0
SWIGLU_ALPHA = 1.702
NEG = -1e30

LANES = 128
SUBLANES = 8
MXU_DIM = 256
VMEM_LIMIT = 56 * 1024 * 1024


def _params(*sem):
    return pltpu.CompilerParams(dimension_semantics=sem, vmem_limit_bytes=VMEM_LIMIT)


def _tile(n, pref):
    return pref if n % pref == 0 else n


def _rms(x, g):
    return x * lax.rsqrt(jnp.mean(x * x, axis=-1, keepdims=True) + EPS) * g


def _dot(a, b):
    return jnp.dot(a, b, preferred_element_type=F32)


def _dot_nt(a, b):
    return lax.dot_general(a, b, (((1,), (1,)), ((), ())), preferred_element_type=F32)


def _dot_tn(a, b):
    return lax.dot_general(a, b, (((0,), (0,)), ((), ())), preferred_element_type=F32)


def _group_meansq(u, gmat):
    sq = u * u
    hi = sq.astype(BF16)
    lo = (sq - hi.astype(F32)).astype(BF16)
    outs = []
    for c in range(u.shape[1] // MXU_DIM):
        sl = slice(c * MXU_DIM, (c + 1) * MXU_DIM)
        outs.append(_dot(hi[:, sl], gmat) + _dot(lo[:, sl], gmat))
    return jnp.concatenate(outs, axis=1) * (1.0 / HEAD_DIM)


def _proj_kernel(x_ref, gmix_ref, w_ref, gqk_ref, gmat_ref, q_ref, kb_ref, vt_ref, k_ref, v_ref, up_ref):
    aw = ATT_WIDTH
    xn = _rms(x_ref[...], gmix_ref[...])
    u = _dot(xn.astype(BF16), w_ref[...])
    qk = u[:, :2 * aw]
    qkn = qk * lax.rsqrt(_group_meansq(qk, gmat_ref[...]) + EPS) * gqk_ref[...]
    kn = qkn[:, aw:]
    v = u[:, 2 * aw:3 * aw]
    for h in range(N_HEADS):
        hs = slice(h * V_DIM, (h + 1) * V_DIM)
        k_ref[:, h, :] = kn[:, hs]
        v_ref[:, h, :] = v[:, hs]
    up_ref[...] = u[:, 3 * aw:]
    q_ref[...] = (qkn[:, :aw] * (HEAD_DIM ** -0.5)).astype(BF16)
    kb_ref[...] = kn.astype(BF16)
    vt_ref[...] = v.T.astype(BF16)


def _proj(x2d, g_mix, w_in_b, gqk, gmat):
    n, d = x2d.shape
    wtot = w_in_b.shape[1]
    aw = ATT_WIDTH
    tm = _tile(n, 512)
    full = lambda shape: pl.BlockSpec(shape, lambda i: (0,) * len(shape))
    rows = lambda w: pl.BlockSpec((tm, w), lambda i: (i, 0))
    heads = pl.BlockSpec((tm, N_HEADS, V_DIM), lambda i: (i, 0, 0))
    return pl.pallas_call(
        _proj_kernel,
        grid=(n // tm,),
        in_specs=[rows(d), full((1, d)), full((d, wtot)), full((1, 2 * aw)), full((MXU_DIM, MXU_DIM))],
        out_specs=[rows(aw), rows(aw), pl.BlockSpec((aw, tm), lambda i: (0, i)), heads, heads, rows(wtot - 3 * aw)],
        out_shape=[jax.ShapeDtypeStruct((n, aw), BF16),
                   jax.ShapeDtypeStruct((n, aw), BF16),
                   jax.ShapeDtypeStruct((aw, n), BF16),
                   jax.ShapeDtypeStruct((n, N_HEADS, V_DIM), F32),
                   jax.ShapeDtypeStruct((n, N_HEADS, V_DIM), F32),
                   jax.ShapeDtypeStruct((n, wtot - 3 * aw), F32)],
        compiler_params=_params("parallel"),
        name="in_proj",
    )(x2d, g_mix, w_in_b, gqk, gmat)


def _lambda_value(lam_ref, lam_init):
    lf = lam_ref[...]
    a = jnp.sum(lf[0:1] * lf[1:2], axis=-1, keepdims=True)
    b = jnp.sum(lf[2:3] * lf[3:4], axis=-1, keepdims=True)
    return jnp.exp(a) - jnp.exp(b) + lam_init


def _sub_norm(od, gsub, lam_init):
    return _rms(od, gsub) * (1.0 - lam_init)


def _softmax_step(s, vb, m, l, acc):
    m_new = jnp.maximum(m, jnp.max(s, axis=-1, keepdims=True))
    alpha = jnp.exp(m - m_new)
    p = jnp.exp(s - m_new)
    l = alpha * l + jnp.sum(p, axis=-1, keepdims=True)
    acc = alpha * acc + _dot(p.astype(BF16), vb)
    return m_new, l, acc


POS_SPLIT = 64


def _attn_kernel(slopes_ref, lam_ref, gsub_ref, q_ref, k_ref, vt_ref, pf_ref, o_ref, acc_ref, *, t, lam_init):
    h = pl.program_id(1)
    i = pl.program_id(2)
    slope = slopes_ref[h]
    q = q_ref[...]
    lane = lax.broadcasted_iota(I32, q.shape, 1)
    zero = jnp.zeros_like(q)
    feat = jnp.where(lane == 0, slope * POS_SPLIT, jnp.where(lane == 1, slope, 0.0)).astype(BF16)
    qa = (jnp.concatenate([jnp.where(lane < HEAD_DIM, q, zero), feat], axis=1),
          jnp.concatenate([jnp.where(lane >= HEAD_DIM, q, zero), feat], axis=1))
    acc_ref[...] = jnp.zeros(acc_ref.shape, F32)

    def block(j, stats, masked):
        k0 = pl.multiple_of(j * t, t)
        kaug = jnp.concatenate([k_ref[pl.ds(k0, t), :], pf_ref[pl.ds(k0, t), :]], axis=1)
        vt = vt_ref[:, pl.ds(k0, t)]
        out = []
        for c in range(2):
            m, l = stats[c]
            s = _dot_nt(kaug, qa[c])
            if masked:
                kr = lax.broadcasted_iota(I32, s.shape, 0)
                qc = lax.broadcasted_iota(I32, s.shape, 1)
                s = jnp.where(kr <= qc, s, NEG)
            m_new = jnp.maximum(m, jnp.max(s, axis=0, keepdims=True))
            alpha = jnp.exp(m - m_new)
            p = jnp.exp(s - m_new)
            l = alpha * l + jnp.sum(p, axis=0, keepdims=True)
            acc_ref[c] = alpha * acc_ref[c] + _dot(vt, p.astype(BF16))
            out.append((m_new, l))
        return tuple(out)

    init = ((jnp.full((1, t), NEG, F32), jnp.zeros((1, t), F32)),) * 2
    stats = lax.fori_loop(0, i, lambda j, st: block(j, st, False), init)
    stats = block(i, stats, True)
    od = acc_ref[0] / stats[0][1] - _lambda_value(lam_ref, lam_init) * (acc_ref[1] / stats[1][1])
    y = od * lax.rsqrt(jnp.mean(od * od, axis=0, keepdims=True) + EPS) * gsub_ref[...] * (1.0 - lam_init)
    o_ref[...] = y.T.astype(o_ref.dtype)


def _attn(q, kb, vt, slopes, lam, gsub_col, seq, lam_init):
    n = q.shape[0]
    b = n // seq
    t = _tile(seq, 512)
    nq = seq // t
    assert seq <= POS_SPLIT * 256, "key positions must split into two bf16-exact factors"
    pos = jnp.arange(seq)
    pf = jnp.zeros((seq, V_DIM), F32).at[:, 0].set(pos // POS_SPLIT).at[:, 1].set(pos % POS_SPLIT).astype(BF16)
    kern = functools.partial(_attn_kernel, t=t, lam_init=lam_init)
    return pl.pallas_call(
        kern,
        grid=(b, N_HEADS, nq),
        in_specs=[pl.BlockSpec(memory_space=pltpu.SMEM),
                  pl.BlockSpec(lam.shape, lambda bi, h, i: (0, 0)),
                  pl.BlockSpec((V_DIM, 1), lambda bi, h, i: (0, 0)),
                  pl.BlockSpec((t, V_DIM), lambda bi, h, i: (bi * nq + i, h)),
                  pl.BlockSpec((seq, V_DIM), lambda bi, h, i: (bi, h)),
                  pl.BlockSpec((V_DIM, seq), lambda bi, h, i: (h, bi)),
                  pl.BlockSpec((seq, V_DIM), lambda bi, h, i: (0, 0))],
        out_specs=pl.BlockSpec((t, V_DIM), lambda bi, h, i: (bi * nq + i, h)),
        out_shape=jax.ShapeDtypeStruct((n, ATT_WIDTH), BF16),
        scratch_shapes=[pltpu.VMEM((2, V_DIM, t), F32)],
        compiler_params=_params("parallel", "parallel", "arbitrary"),
        name="prompt_attn",
    )(slopes, lam, gsub_col, q, kb, vt, pf)


def _page_copies(pt_ref, ck_hbm, cv_hbm, kbuf, vbuf, sem, step, slot, *, layer, n_chunks, pages_per_chunk, page):
    b = step // n_chunks
    c = step % n_chunks
    n_pages = n_chunks * pages_per_chunk
    copies = []
    for p in range(pages_per_chunk):
        phys = pt_ref[b * n_pages + c * pages_per_chunk + p]
        dst = pl.ds(p * page, page)
        copies.append(pltpu.make_async_copy(ck_hbm.at[layer, phys], kbuf.at[slot, dst], sem.at[0, slot]))
        copies.append(pltpu.make_async_copy(cv_hbm.at[layer, phys], vbuf.at[slot, dst], sem.at[1, slot]))
    return copies


def _heads_to_lanes(buf, slot):
    return jnp.concatenate([buf[slot, :, hh, :] for hh in range(N_HEADS)], axis=1).astype(BF16)


def _sample_attn_kernel(pt_ref, slopes_ref, lam_ref, gsub_ref, q_ref, kn_ref, vn_ref, ck_hbm, cv_hbm, o_ref,
                        kbuf, vbuf, sem, m_ref, l_ref, acc_ref, *, layer, n_chunks, pages_per_chunk, page, t_new,
                        lam_init):
    b = pl.program_id(0)
    c = pl.program_id(1)
    step = b * n_chunks + c
    n_steps = pl.num_programs(0) * n_chunks
    slot = step % 2
    copies = functools.partial(_page_copies, pt_ref, ck_hbm, cv_hbm, kbuf, vbuf, sem, layer=layer,
                               n_chunks=n_chunks, pages_per_chunk=pages_per_chunk, page=page)

    @pl.when(step == 0)
    def _():
        for cp in copies(step, slot):
            cp.start()

    @pl.when(step + 1 < n_steps)
    def _():
        for cp in copies(step + 1, 1 - slot):
            cp.start()

    @pl.when(c == 0)
    def _():
        m_ref[...] = jnp.full(m_ref.shape, NEG, F32)
        l_ref[...] = jnp.zeros(l_ref.shape, F32)
        acc_ref[...] = jnp.zeros(acc_ref.shape, F32)

    qz = q_ref[0]
    n_rows = qz.shape[0]
    ck = pages_per_chunk * page
    past = n_chunks * ck
    rows = lax.broadcasted_iota(I32, (n_rows, 1), 0)
    tok = rows % t_new
    head = rows // (2 * t_new)
    slope = jnp.zeros((n_rows, 1), F32)
    for hh in range(N_HEADS):
        slope = jnp.where(head == hh, slopes_ref[hh], slope)

    for cp in copies(step, slot):
        cp.wait()

    kb = _heads_to_lanes(kbuf, slot)
    vb = _heads_to_lanes(vbuf, slot)
    cols = lax.broadcasted_iota(I32, (n_rows, ck), 1)
    dist = (past + tok) - (c * ck + cols)
    s = _dot_nt(qz, kb) - slope * dist.astype(F32)
    m, l, acc = _softmax_step(s, vb, m_ref[...], l_ref[...], acc_ref[...])
    m_ref[...] = m
    l_ref[...] = l
    acc_ref[...] = acc

    @pl.when(c == n_chunks - 1)
    def _():
        knb = kn_ref[0].astype(BF16)
        vnb = vn_ref[0].astype(BF16)
        ncol = lax.broadcasted_iota(I32, (n_rows, knb.shape[0]), 1)
        nd = tok - ncol
        sn = _dot_nt(qz, knb) - slope * nd.astype(F32)
        sn = jnp.where((nd >= 0) & (ncol < t_new), sn, NEG)
        _, l2, acc2 = _softmax_step(sn, vnb, m, l, acc)
        o = acc2 / l2
        lanes = lax.broadcasted_iota(I32, o.shape, 1)
        o = jnp.where(lanes // V_DIM == head, o, 0.0)
        r = o.reshape(N_HEADS, 2 * t_new, o.shape[1]).sum(axis=0)
        od = r - _lambda_value(lam_ref, lam_init) * pltpu.roll(r, t_new, axis=0)
        outs = [_sub_norm(od[:, hh * V_DIM:(hh + 1) * V_DIM], gsub_ref[...], lam_init) for hh in range(N_HEADS)]
        o_ref[0] = jnp.concatenate(outs, axis=1).astype(o_ref.dtype)


def _sample_attn(q, kn, vn, cache_k, cache_v, layer, page_table, slopes, lam, gsub, lam_init):
    b, t_new, width = q.shape
    n_pages = page_table.shape[1]
    page = cache_k.shape[2]
    ppc = 8 if n_pages % 8 == 0 else n_pages
    n_chunks = n_pages // ppc
    n_rows = N_HEADS * 2 * t_new
    assert 2 * t_new == SUBLANES, "row regrouping assumes two maps of four tokens fill one sublane tile"
    lane_grp = (jnp.arange(width) // HEAD_DIM)[None, :]
    row_grp = jnp.arange(2 * N_HEADS)[:, None]
    mask = (lane_grp == row_grp).astype(q.dtype)
    qz = (q[:, None, :, :] * mask[None, :, None, :]).reshape(b, n_rows, width)
    pad = lambda a: jnp.pad(a, ((0, 0), (0, SUBLANES - t_new), (0, 0)))
    kern = functools.partial(_sample_attn_kernel, layer=layer, n_chunks=n_chunks, pages_per_chunk=ppc, page=page,
                             t_new=t_new, lam_init=lam_init)
    grid_spec = pltpu.PrefetchScalarGridSpec(
        num_scalar_prefetch=1,
        grid=(b, n_chunks),
        in_specs=[pl.BlockSpec(memory_space=pltpu.SMEM),
                  pl.BlockSpec(lam.shape, lambda bi, c, pt: (0, 0)),
                  pl.BlockSpec((1, V_DIM), lambda bi, c, pt: (0, 0)),
                  pl.BlockSpec((1, n_rows, width), lambda bi, c, pt: (bi, 0, 0)),
                  pl.BlockSpec((1, SUBLANES, width), lambda bi, c, pt: (bi, 0, 0)),
                  pl.BlockSpec((1, SUBLANES, width), lambda bi, c, pt: (bi, 0, 0)),
                  pl.BlockSpec(memory_space=pl.ANY),
                  pl.BlockSpec(memory_space=pl.ANY)],
        out_specs=pl.BlockSpec((1, SUBLANES, width), lambda bi, c, pt: (bi, 0, 0)),
        scratch_shapes=[pltpu.VMEM((2, ppc * page, N_HEADS, V_DIM), F32),
                        pltpu.VMEM((2, ppc * page, N_HEADS, V_DIM), F32),
                        pltpu.SemaphoreType.DMA((2, 2)),
                        pltpu.VMEM((n_rows, 1), F32),
                        pltpu.VMEM((n_rows, 1), F32),
                        pltpu.VMEM((n_rows, width), F32)])
    out = pl.pallas_call(
        kern,
        grid_spec=grid_spec,
        out_shape=jax.ShapeDtypeStruct((b, SUBLANES, width), BF16),
        compiler_params=_params("arbitrary", "arbitrary"),
        name="sample_attn",
    )(page_table.reshape(-1), slopes, lam, gsub, qz, pad(kn), pad(vn), cache_k, cache_v)
    return out[:, :t_new]


def _outproj_kernel(x_ref, oatt_ref, up_ref, halo_ref, wbd_ref, pscale_ref, wout_ref, o_ref, ext_ref, *,
                    tm, n_prev, zero_first_halo):
    i = pl.program_id(1)
    halo = halo_ref[0]
    if zero_first_halo:
        halo = jnp.where(i == 0, 0.0, halo)
    cur = up_ref[0]
    ext_ref[0:HALO, :] = halo
    ext_ref[HALO:HALO + tm, :] = cur
    gw = cur.shape[1] // len(POOL_WINDOWS)
    pos1 = lax.broadcasted_iota(I32, (tm, gw), 0) + (i * tm + n_prev + 1)
    ds = []
    for g, w in enumerate(POOL_WINDOWS):
        cs = slice(g * gw, (g + 1) * gw)
        acc = cur[:, cs]
        for sft in range(1, w):
            acc = acc + ext_ref[HALO - sft:HALO - sft + tm, cs]
        cnt = jnp.minimum(pos1, w).astype(F32)
        ds.append(acc / cnt - cur[:, cs])
    d = jnp.concatenate(ds, axis=1)
    o_pool = _dot(d.astype(BF16), wbd_ref[...]) * pscale_ref[...]
    aw = oatt_ref.shape[2]
    o_ref[0] = (x_ref[0] + _dot(oatt_ref[0], wout_ref[0:aw, :])
                + _dot(o_pool.astype(BF16), wout_ref[aw:, :]))


def _outproj(x, o_att, up, halo_arr, wbd, pool_scale, w_out_b, *, n_prev, halo_from_up):
    b, t, d = x.shape
    aw = o_att.shape[2]
    pw = up.shape[2]
    tm = _tile(t, 512)
    if halo_from_up:
        halo_map = lambda bi, i: (bi, jnp.maximum(i * (tm // HALO) - 1, 0), 0)
    else:
        halo_map = lambda bi, i: (bi, 0, 0)
    kern = functools.partial(_outproj_kernel, tm=tm, n_prev=n_prev, zero_first_halo=halo_from_up)
    full = lambda shape: pl.BlockSpec(shape, lambda bi, i: (0,) * len(shape))
    return pl.pallas_call(
        kern,
        grid=(b, t // tm),
        in_specs=[pl.BlockSpec((1, tm, d), lambda bi, i: (bi, i, 0)),
                  pl.BlockSpec((1, tm, aw), lambda bi, i: (bi, i, 0)),
                  pl.BlockSpec((1, tm, pw), lambda bi, i: (bi, i, 0)),
                  pl.BlockSpec((1, HALO, pw), halo_map),
                  full((pw, pw)), full((1, pw)), full((aw + pw, d))],
        out_specs=pl.BlockSpec((1, tm, d), lambda bi, i: (bi, i, 0)),
        out_shape=jax.ShapeDtypeStruct((b, t, d), F32),
        scratch_shapes=[pltpu.VMEM((HALO + tm, pw), F32)],
        compiler_params=_params("parallel", "arbitrary"),
        name="pool_out_proj",
    )(x, o_att, up, halo_arr, wbd, pool_scale, w_out_b)


def _memkv_kernel(mem_ref, gmem_ref, w_ref, gxk_ref, gmat_ref, mk_ref, mv_ref):
    xw = mk_ref.shape[1]
    kv = _dot(_rms(mem_ref[...], gmem_ref[...]).astype(BF16), w_ref[...])
    kx = kv[:, :xw]
    mk_ref[...] = kx * lax.rsqrt(_group_meansq(kx, gmat_ref[...]) + EPS) * gxk_ref[...]
    mv_ref[...] = kv[:, xw:]


def _memkv(mem2d, g_mem, w_xkv_b, gxk, gmat):
    n, d = mem2d.shape
    xw = w_xkv_b.shape[1] // 2
    tm = _tile(n, 512)
    full = lambda shape: pl.BlockSpec(shape, lambda i: (0,) * len(shape))
    return pl.pallas_call(
        _memkv_kernel,
        grid=(n // tm,),
        in_specs=[pl.BlockSpec((tm, d), lambda i: (i, 0)), full((1, d)), full((d, 2 * xw)),
                  full((1, xw)), full((MXU_DIM, MXU_DIM))],
        out_specs=[pl.BlockSpec((tm, xw), lambda i: (i, 0))] * 2,
        out_shape=[jax.ShapeDtypeStruct((n, xw), F32)] * 2,
        compiler_params=_params("parallel"),
        name="mem_kv",
    )(mem2d, g_mem, w_xkv_b, gxk, gmat)


def _xattn_kernel(h_ref, mk_ref, mv_ref, gx_ref, wq_ref, gxq_ref, gmat_ref, wo_ref, o_ref):
    h1 = h_ref[0]
    qx = _dot(_rms(h1, gx_ref[...]).astype(BF16), wq_ref[...])
    qn = (qx * lax.rsqrt(_group_meansq(qx, gmat_ref[...]) + EPS) * gxq_ref[...]).astype(BF16)
    mk = mk_ref[0].astype(BF16)
    mv = mv_ref[0].astype(BF16)
    head_of_lane = lax.broadcasted_iota(I32, qn.shape, 1) // X_HEAD_DIM
    o = jnp.zeros(qn.shape, F32)
    for hh in range(X_HEADS):
        sel = head_of_lane == hh
        s = _dot_nt(jnp.where(sel, qn, jnp.zeros_like(qn)), mk)
        p = jnp.exp(s - jnp.max(s, axis=-1, keepdims=True))
        p = p / jnp.sum(p, axis=-1, keepdims=True)
        o = o + jnp.where(sel, _dot(p.astype(BF16), mv), 0.0)
    o_ref[0] = h1 + _dot(o.astype(BF16), wo_ref[...])


def _xattn(h1, mk, mv, g_xattn, w_xq_b, gxq, gmat, w_xo_b):
    b, t, d = h1.shape
    nm, xw = mk.shape[1], mk.shape[2]
    tq = _tile(t, 512)
    full = lambda shape: pl.BlockSpec(shape, lambda bi, i: (0,) * len(shape))
    return pl.pallas_call(
        _xattn_kernel,
        grid=(b, t // tq),
        in_specs=[pl.BlockSpec((1, tq, d), lambda bi, i: (bi, i, 0)),
                  pl.BlockSpec((1, nm, xw), lambda bi, i: (bi, 0, 0)),
                  pl.BlockSpec((1, nm, xw), lambda bi, i: (bi, 0, 0)),
                  full((1, d)), full((d, xw)), full((1, xw)), full((MXU_DIM, MXU_DIM)), full((xw, d))],
        out_specs=pl.BlockSpec((1, tq, d), lambda bi, i: (bi, i, 0)),
        out_shape=jax.ShapeDtypeStruct((b, t, d), F32),
        compiler_params=_params("parallel", "parallel"),
        name="mem_xattn",
    )(h1, mk, mv, g_xattn, w_xq_b, gxq, gmat, w_xo_b)


def _gather_rows(t, n_experts):
    return -(-(TOP_K * t + n_experts * (SUBLANES - 1)) // LANES) * LANES


def _slot_onehot(pos, n_rows):
    t = pos.shape[0]
    r = lax.broadcasted_iota(I32, (t, n_rows), 1)
    hit = r == pos[:, 0:1]
    for k in range(1, TOP_K):
        hit = hit | (r == pos[:, k:k + 1])
    return jnp.where(hit, 1.0, 0.0).astype(BF16)


def _route_kernel(h_ref, gffn_ref, wr_ref, br_ref, xg_ref, pos_ref, cnt_ref, *, n_experts):
    t = h_ref.shape[0]
    n_rows = xg_ref.shape[1]
    mb = _rms(h_ref[...], gffn_ref[...]).astype(BF16)
    logits = _dot(mb, wr_ref[...]) + br_ref[...]
    lane = lax.broadcasted_iota(I32, logits.shape, 1).astype(F32)
    work = logits
    sels, vals = [], []
    for _ in range(TOP_K):
        mx = jnp.max(work, axis=-1, keepdims=True)
        idx = jnp.min(jnp.where(work == mx, lane, float(LANES)), axis=-1, keepdims=True)
        sel = lane == idx
        sels.append(sel)
        vals.append(mx)
        work = jnp.where(sel, 2.0 * NEG, work)
    es = [jnp.exp(v - vals[0]) for v in vals]
    denom = es[0]
    for e in es[1:]:
        denom = denom + e
    hot = jnp.zeros(logits.shape, F32)
    gate_m = jnp.zeros(logits.shape, F32)
    for sel, e in zip(sels, es):
        hot = jnp.where(sel, 1.0, hot)
        gate_m = jnp.where(sel, e / denom, gate_m)
    ri = lax.broadcasted_iota(I32, (t, t), 0)
    ci = lax.broadcasted_iota(I32, (t, t), 1)
    rank = _dot(jnp.where(ci < ri, 1.0, 0.0).astype(BF16), hot.astype(BF16))
    count = jnp.sum(hot, axis=0, keepdims=True).astype(I32)
    c8 = (((count + (SUBLANES - 1)) >> 3) << 3).astype(F32)
    ei = lax.broadcasted_iota(I32, (LANES, LANES), 0)
    ej = lax.broadcasted_iota(I32, (LANES, LANES), 1)
    off = _dot(jnp.broadcast_to(c8, (SUBLANES, LANES)).astype(BF16), jnp.where(ei < ej, 1.0, 0.0).astype(BF16))[0:1]
    slot = off + rank
    lane_i = lax.broadcasted_iota(I32, logits.shape, 1)
    pos = jnp.full(logits.shape, -1, I32)
    for k, sel in enumerate(sels):
        pk = jnp.sum(jnp.where(sel, slot, 0.0), axis=-1, keepdims=True).astype(I32)
        pos = jnp.where(lane_i == k, pk, pos)
    pos_ref[...] = pos
    cnt_ref[0] = jnp.broadcast_to(count, (SUBLANES, LANES))
    g1 = gate_m.astype(BF16)
    r1 = gate_m - g1.astype(F32)
    g2 = r1.astype(BF16)
    g3 = (r1 - g2.astype(F32)).astype(BF16)
    gx = (g1.astype(F32) + pltpu.roll(g2.astype(F32), n_experts, axis=1)
          + pltpu.roll(g3.astype(F32), 2 * n_experts, axis=1)).astype(BF16)
    xg_ref[0] = _dot_tn(_slot_onehot(pos, n_rows), jnp.concatenate([mb, gx], axis=1))


def _route(h2d, g_ffn, wr_pad_b, br_pad, n_experts):
    n, d = h2d.shape
    t = _tile(n, 256)
    n_sub = n // t
    n_rows = _gather_rows(t, n_experts)
    kern = functools.partial(_route_kernel, n_experts=n_experts)
    full = lambda shape: pl.BlockSpec(shape, lambda i: (0,) * len(shape))
    return pl.pallas_call(
        kern,
        grid=(n_sub,),
        in_specs=[pl.BlockSpec((t, d), lambda i: (i, 0)), full((1, d)), full((d, LANES)), full((1, LANES))],
        out_specs=[pl.BlockSpec((1, n_rows, d + LANES), lambda i: (i, 0, 0)),
                   pl.BlockSpec((t, LANES), lambda i: (i, 0)),
                   pl.BlockSpec((1, SUBLANES, LANES), lambda i: (i, 0, 0))],
        out_shape=[jax.ShapeDtypeStruct((n_sub, n_rows, d + LANES), F32),
                   jax.ShapeDtypeStruct((n, LANES), I32),
                   jax.ShapeDtypeStruct((n_sub, SUBLANES, LANES), I32)],
        compiler_params=_params("parallel"),
        name="moe_route_gather",
    )(h2d, g_ffn, wr_pad_b, br_pad)


def _unit_plan(cnt, n_rows, n_experts):
    n_sub = cnt.shape[0]
    upt = n_rows // SUBLANES
    u = (cnt + (SUBLANES - 1)) // SUBLANES
    uoff = jnp.cumsum(u, axis=1) - u
    used = jnp.sum(u, axis=1, keepdims=True)
    u_all = jnp.concatenate([u, upt - used], axis=1)
    uoff_all = jnp.concatenate([uoff, used], axis=1)
    flat_cnt = u_all.T.reshape(-1)
    cum = jnp.cumsum(flat_cnt)
    start = cum - flat_cnt
    j = jnp.arange(n_sub * upt, dtype=I32)
    cum2d = cum.reshape(n_experts + 1, n_sub)
    e_of = jnp.sum(cum2d[None, :, -1] <= j[:, None], axis=1)
    tile_of = jnp.sum(cum2d[e_of] <= j[:, None], axis=1)
    chunk = (e_of * n_sub + tile_of).astype(I32)
    base = tile_of * upt + uoff_all.T.reshape(-1)[chunk]
    unit_src = (base + (j - start[chunk])).astype(I32)
    ustart = jnp.concatenate([start[::n_sub], cum[-1:]]).astype(I32)
    return ustart, unit_src


def _expert_kernel(ustart_ref, usrc_ref, xg_hbm, wgu_ref, bgu_ref, wd_ref, bd_ref, yg_hbm,
                   xbuf, ybuf, wgu_b, wd_b, in_sem, out_sem, *, n_experts, upb, d_model):
    e = pl.program_id(0)
    u0 = ustart_ref[e]
    u1 = ustart_ref[e + 1]
    dff = wd_b.shape[0]

    def in_copy(j, slot, i):
        return pltpu.make_async_copy(xg_hbm.at[usrc_ref[j]], xbuf.at[slot, pl.ds(i * SUBLANES, SUBLANES)],
                                     in_sem.at[slot])

    def out_copy(j, i):
        return pltpu.make_async_copy(ybuf.at[pl.ds(i * SUBLANES, SUBLANES)], yg_hbm.at[usrc_ref[j]], out_sem.at[0])

    def n_valid(blk):
        return jnp.minimum(upb, u1 - (u0 + blk * upb))

    def for_units(blk, fn):
        base = u0 + blk * upb
        nv = n_valid(blk)

        @pl.when(nv == upb)
        def _():
            for i in range(upb):
                fn(base + i, i)

        @pl.when(nv < upb)
        def _():
            def body(i, c):
                fn(base + i, i)
                return c
            lax.fori_loop(0, nv, body, 0)

    @pl.when(e == 0)
    def _():
        xbuf[...] = jnp.zeros(xbuf.shape, F32)

    @pl.when(e < n_experts)
    def _():
        rc = 128
        for r in range(0, wgu_b.shape[0], rc):
            wgu_b[r:r + rc, :] = wgu_ref[0, r:r + rc, :].astype(BF16)
        for r in range(0, wd_b.shape[0], rc):
            wd_b[r:r + rc, :] = wd_ref[0, r:r + rc, :].astype(BF16)
        n_blk = (u1 - u0 + upb - 1) // upb

        @pl.when(n_blk > 0)
        def _():
            for_units(0, lambda j, i: in_copy(j, 0, i).start())

        def block(blk, c):
            slot = blk % 2

            @pl.when(blk + 1 < n_blk)
            def _():
                for_units(blk + 1, lambda j, i: in_copy(j, 1 - slot, i).start())

            for_units(blk, lambda j, i: in_copy(j, slot, i).wait())
            x = xbuf[slot]
            gcols = x[:, d_model:]
            lane = lax.broadcasted_iota(I32, gcols.shape, 1)
            own = (lane % n_experts == e) & (lane < 3 * n_experts)
            gate_w = jnp.sum(jnp.where(own, gcols, 0.0), axis=-1, keepdims=True)
            xb = x[:, :d_model].astype(BF16)
            hidden = []
            for c0 in range(0, dff, MXU_DIM):
                gs = slice(c0, c0 + MXU_DIM)
                us = slice(dff + c0, dff + c0 + MXU_DIM)
                gate = jnp.minimum(_dot(xb, wgu_b[:, gs]) + bgu_ref[0, :, gs], SWIGLU_LIMIT)
                up = jnp.clip(_dot(xb, wgu_b[:, us]) + bgu_ref[0, :, us], -SWIGLU_LIMIT, SWIGLU_LIMIT)
                glu = gate * (1.0 / (1.0 + jnp.exp(-SWIGLU_ALPHA * gate)))
                hidden.append(((up + 1.0) * glu).astype(BF16))
            y = (_dot(jnp.concatenate(hidden, axis=1), wd_b[...]) + bd_ref[0]) * gate_w

            @pl.when(blk > 0)
            def _():
                for_units(blk - 1, lambda j, i: out_copy(j, i).wait())

            ybuf[...] = y
            for_units(blk, lambda j, i: out_copy(j, i).start())
            return c

        lax.fori_loop(0, n_blk, block, 0)

        @pl.when(n_blk > 0)
        def _():
            for_units(n_blk - 1, lambda j, i: out_copy(j, i).wait())

    @pl.when(e == n_experts)
    def _():
        ybuf[...] = jnp.zeros(ybuf.shape, F32)

        def start(j, c):
            out_copy(j, 0).start()
            return c

        def wait(j, c):
            out_copy(j, 0).wait()
            return c
        lax.fori_loop(u0, u1, start, 0)
        lax.fori_loop(u0, u1, wait, 0)


def _experts(ustart, unit_src, xg, w_gu, b_gu, w_d, b_d, tr):
    n_sub, n_rows, dx = xg.shape
    n_experts, d_model, dff2 = w_gu.shape
    dff = w_d.shape[1]
    n_units = n_sub * n_rows // SUBLANES
    upb = tr // SUBLANES
    kern = functools.partial(_expert_kernel, n_experts=n_experts, upb=upb, d_model=d_model)
    last = n_experts - 1
    grid_spec = pltpu.PrefetchScalarGridSpec(
        num_scalar_prefetch=2,
        grid=(n_experts + 1,),
        in_specs=[pl.BlockSpec(memory_space=pl.ANY),
                  pl.BlockSpec((1, d_model, dff2), lambda e, us, src: (jnp.minimum(e, last), 0, 0)),
                  pl.BlockSpec((1, 1, dff2), lambda e, us, src: (jnp.minimum(e, last), 0, 0)),
                  pl.BlockSpec((1, dff, d_model), lambda e, us, src: (jnp.minimum(e, last), 0, 0)),
                  pl.BlockSpec((1, 1, d_model), lambda e, us, src: (jnp.minimum(e, last), 0, 0))],
        out_specs=pl.BlockSpec(memory_space=pl.ANY),
        scratch_shapes=[pltpu.VMEM((2, tr, dx), F32),
                        pltpu.VMEM((tr, d_model), F32),
                        pltpu.VMEM((d_model, dff2), BF16),
                        pltpu.VMEM((dff, d_model), BF16),
                        pltpu.SemaphoreType.DMA((2,)),
                        pltpu.SemaphoreType.DMA((1,))])
    yg = pl.pallas_call(
        kern,
        grid_spec=grid_spec,
        out_shape=jax.ShapeDtypeStruct((n_units, SUBLANES, d_model), F32),
        compiler_params=_params("arbitrary"),
        name="moe_experts",
    )(ustart, unit_src, xg.reshape(n_units, SUBLANES, dx), w_gu, b_gu.reshape(n_experts, 1, dff2),
      w_d, b_d.reshape(n_experts, 1, d_model))
    return yg.reshape(n_sub, n_rows, d_model)


def _combine_kernel(yg_ref, pos_ref, h_ref, o_ref):
    onehot = _slot_onehot(pos_ref[...], yg_ref.shape[1])
    o_ref[...] = h_ref[...] + _dot(onehot, yg_ref[0].astype(BF16))


def _combine(yg, pos, h2d):
    n_sub, n_rows, d = yg.shape
    n = h2d.shape[0]
    t = n // n_sub
    return pl.pallas_call(
        _combine_kernel,
        grid=(n_sub,),
        in_specs=[pl.BlockSpec((1, n_rows, d), lambda i: (i, 0, 0)),
                  pl.BlockSpec((t, LANES), lambda i: (i, 0)),
                  pl.BlockSpec((t, d), lambda i: (i, 0))],
        out_specs=pl.BlockSpec((t, d), lambda i: (i, 0)),
        out_shape=jax.ShapeDtypeStruct((n, d), F32),
        compiler_params=_params("parallel"),
        name="moe_combine",
    )(yg, pos, h2d)


def _moe(h, g_ffn, wr_pad_b, br_pad, w_gu, b_gu, w_d, b_d):
    b, t, d = h.shape
    n_experts = w_gu.shape[0]
    h2d = h.reshape(b * t, d)
    xg, pos, cnt = _route(h2d, g_ffn, wr_pad_b, br_pad, n_experts)
    ustart, unit_src = _unit_plan(cnt[:, 0, :n_experts], xg.shape[1], n_experts)
    tr = 512 if b * t >= 4096 else 128
    yg = _experts(ustart, unit_src, xg, w_gu, b_gu, w_d, b_d, tr)
    return _combine(yg, pos, h2d).reshape(b, t, d)


def kernel(x_prompt, x_sample, mem_prompt, cache_k, cache_v, cache_mem_k, cache_mem_v, state_pool, page_table,
           g_mix, w_in, g_q, g_k, lam, g_sub, w_pool, pool_scale, w_out,
           g_xattn, g_mem, w_xq, w_xkv, g_xq, g_xk, w_xo,
           g_ffn, w_router, b_router, w_gate_up, b_gate_up, w_down, b_down):
    depth = w_in.shape[0]
    b_p, s_p, d = x_prompt.shape
    b_s, t_s, _ = x_sample.shape
    n_mem = mem_prompt.shape[1]
    n_experts = w_router.shape[2]
    assert LANES % n_experts == 0 and 3 * n_experts <= LANES
    assert ATT_WIDTH % MXU_DIM == 0 and (X_HEADS * X_HEAD_DIM) % MXU_DIM == 0

    slopes = jnp.asarray([2.0 ** (-8.0 * (h + 1) / N_HEADS) for h in range(N_HEADS)], F32)
    grp = jnp.arange(MXU_DIM) // HEAD_DIM
    gmat = (grp[:, None] == grp[None, :]).astype(BF16)
    row = lambda a: a.reshape(1, -1).astype(F32)

    hp, hs = x_prompt, x_sample
    outs = [[] for _ in range(8)]
    for l in range(depth):
        lam_init = 0.8 - 0.6 * math.exp(-0.3 * l)
        w_in_b = w_in[l].astype(BF16)
        w_out_b = w_out[l].astype(BF16)
        w_xq_b = w_xq[l].astype(BF16)
        w_xkv_b = w_xkv[l].astype(BF16)
        w_xo_b = w_xo[l].astype(BF16)
        gqk = jnp.concatenate([jnp.tile(g_q[l].reshape(-1), N_HEADS), jnp.tile(g_k[l].reshape(-1), N_HEADS)])[None, :]
        gxq = (jnp.tile(g_xq[l], X_HEADS) * (X_HEAD_DIM ** -0.5))[None, :]
        gxk = jnp.tile(g_xk[l], X_HEADS)[None, :]
        wbd = jax.scipy.linalg.block_diag(*[w_pool[l, g] for g in range(len(POOL_WINDOWS))]).astype(BF16)
        wr_pad_b = jnp.pad(w_router[l], ((0, 0), (0, LANES - n_experts))).astype(BF16)
        br_pad = jnp.pad(b_router[l].astype(F32), (0, LANES - n_experts), constant_values=NEG)[None, :]
        moe_w = (row(g_ffn[l]), wr_pad_b, br_pad, w_gate_up[l], b_gate_up[l], w_down[l], b_down[l])
        gsub = row(g_sub[l])

        q_b, k_b, v_t, k32, v32, up = _proj(hp.reshape(b_p * s_p, d), row(g_mix[l]), w_in_b, gqk, gmat)
        o_att = _attn(q_b, k_b, v_t, slopes, lam[l], g_sub[l].reshape(-1, 1).astype(F32), s_p, lam_init)
        up3 = up.reshape(b_p, s_p, -1)
        h1 = _outproj(hp, o_att.reshape(b_p, s_p, -1), up3, up3, wbd, row(pool_scale[l]), w_out_b, n_prev=0,
                      halo_from_up=True)
        mk, mv = _memkv(mem_prompt.reshape(b_p * n_mem, d), row(g_mem[l]), w_xkv_b, gxk, gmat)
        h2 = _xattn(h1, mk.reshape(b_p, n_mem, -1), mv.reshape(b_p, n_mem, -1), row(g_xattn[l]), w_xq_b, gxq, gmat,
                    w_xo_b)
        hp = _moe(h2, *moe_w)
        outs[0].append(k32.reshape(b_p, s_p, N_HEADS, V_DIM))
        outs[1].append(v32.reshape(b_p, s_p, N_HEADS, V_DIM))
        outs[2].append(up3[:, s_p - POOL_BUF:])
        outs[3].append(mk.reshape(b_p, n_mem, X_HEADS, X_HEAD_DIM))
        outs[4].append(mv.reshape(b_p, n_mem, X_HEADS, X_HEAD_DIM))

        q_s, _, _, k32_s, v32_s, up_s = _proj(hs.reshape(b_s * t_s, d), row(g_mix[l]), w_in_b, gqk, gmat)
        kn = k32_s.reshape(b_s, t_s, -1)
        vn = v32_s.reshape(b_s, t_s, -1)
        o_att_s = _sample_attn(q_s.reshape(b_s, t_s, -1), kn, vn, cache_k, cache_v, l, page_table, slopes, lam[l],
                               gsub, lam_init)
        ups = up_s.reshape(b_s, t_s, -1)
        halo_s = jnp.pad(state_pool[l], ((0, 0), (HALO - POOL_BUF, 0), (0, 0)))
        h1s = _outproj(hs, o_att_s, ups, halo_s, wbd, row(pool_scale[l]), w_out_b, n_prev=POOL_BUF, halo_from_up=False)
        h2s = _xattn(h1s, cache_mem_k[l].reshape(b_s, n_mem, -1), cache_mem_v[l].reshape(b_s, n_mem, -1),
                     row(g_xattn[l]), w_xq_b, gxq, gmat, w_xo_b)
        hs = _moe(h2s, *moe_w)
        outs[5].append(kn.reshape(b_s, t_s, N_HEADS, V_DIM))
        outs[6].append(vn.reshape(b_s, t_s, N_HEADS, V_DIM))
        outs[7].append(jnp.concatenate([state_pool[l], ups], axis=1)[:, t_s:])

    stacked = [jnp.stack(o) for o in outs]
    return (hp, hs, *stacked)
```

```python
import functools
import math

import jax
import jax.numpy as jnp
from jax import lax
from jax.experimental import pallas as pl
from jax.experimental.pallas import tpu as pltpu

F32 = jnp.float32
BF16 = jnp.bfloat16
I32 = jnp.int32

EPS = 1e-6
N_HEADS = 4
HEAD_DIM = 64
V_DIM = 2 * HEAD_DIM
ATT_WIDTH = N_HEADS * V_DIM
POOL_WINDOWS = (2, 4, 8, 16)
POOL_BUF = max(POOL_WINDOWS) - 1
HALO = POOL_BUF + 1
X_HEADS = 4
X_HEAD_DIM = 64
TOP_K = 4
SWIGLU_LIMIT = 7.0
SWIGLU_ALPHA = 1.702
NEG = -1e30

LANES = 128
SUBLANES = 8
MXU_DIM = 256
VMEM_LIMIT = 56 * 1024 * 1024


def _params(*sem):
    return pltpu.CompilerParams(dimension_semantics=sem, vmem_limit_bytes=VMEM_LIMIT)


def _tile(n, pref):
    return pref if n % pref == 0 else n


def _rms(x, g):
    return x * lax.rsqrt(jnp.mean(x * x, axis=-1, keepdims=True) + EPS) * g


def _dot(a, b):
    return jnp.dot(a, b, preferred_element_type=F32)


def _dot_nt(a, b):
    return lax.dot_general(a, b, (((1,), (1,)), ((), ())), preferred_element_type=F32)


def _dot_tn(a, b):
    return lax.dot_general(a, b, (((0,), (0,)), ((), ())), preferred_element_type=F32)


def _group_meansq(u, gmat):
    sq = u * u
    hi = sq.astype(BF16)
    lo = (sq - hi.astype(F32)).astype(BF16)
    outs = []
    for c in range(u.shape[1] // MXU_DIM):
        sl = slice(c * MXU_DIM, (c + 1) * MXU_DIM)
        outs.append(_dot(hi[:, sl], gmat) + _dot(lo[:, sl], gmat))
    return jnp.concatenate(outs, axis=1) * (1.0 / HEAD_DIM)


def _proj_kernel(x_ref, gmix_ref, w_ref, gqk_ref, gmat_ref, q_ref, kb_ref, vt_ref, k_ref, v_ref, up_ref):
    aw = ATT_WIDTH
    xn = _rms(x_ref[...], gmix_ref[...])
    u = _dot(xn.astype(BF16), w_ref[...])
    qk = u[:, :2 * aw]
    qkn = qk * lax.rsqrt(_group_meansq(qk, gmat_ref[...]) + EPS) * gqk_ref[...]
    kn = qkn[:, aw:]
    v = u[:, 2 * aw:3 * aw]
    for h in range(N_HEADS):
        hs = slice(h * V_DIM, (h + 1) * V_DIM)
        k_ref[:, h, :] = kn[:, hs]
        v_ref[:, h, :] = v[:, hs]
    up_ref[...] = u[:, 3 * aw:]
    q_ref[...] = (qkn[:, :aw] * (HEAD_DIM ** -0.5)).astype(BF16)
    kb_ref[...] = kn.astype(BF16)
    vt_ref[...] = v.T.astype(BF16)


def _proj(x2d, g_mix, w_in_b, gqk, gmat):
    n, d = x2d.shape
    wtot = w_in_b.shape[1]
    aw = ATT_WIDTH
    tm = _tile(n, 512)
    full = lambda shape: pl.BlockSpec(shape, lambda i: (0,) * len(shape))
    rows = lambda w: pl.BlockSpec((tm, w), lambda i: (i, 0))
    heads = pl.BlockSpec((tm, N_HEADS, V_DIM), lambda i: (i, 0, 0))
    return pl.pallas_call(
        _proj_kernel,
        grid=(n // tm,),
        in_specs=[rows(d), full((1, d)), full((d, wtot)), full((1, 2 * aw)), full((MXU_DIM, MXU_DIM))],
        out_specs=[rows(aw), rows(aw), pl.BlockSpec((aw, tm), lambda i: (0, i)), heads, heads, rows(wtot - 3 * aw)],
        out_shape=[jax.ShapeDtypeStruct((n, aw), BF16),
                   jax.ShapeDtypeStruct((n, aw), BF16),
                   jax.ShapeDtypeStruct((aw, n), BF16),
                   jax.ShapeDtypeStruct((n, N_HEADS, V_DIM), F32),
                   jax.ShapeDtypeStruct((n, N_HEADS, V_DIM), F32),
                   jax.ShapeDtypeStruct((n, wtot - 3 * aw), F32)],
        compiler_params=_params("parallel"),
        name="in_proj",
    )(x2d, g_mix, w_in_b, gqk, gmat)


def _lambda_value(lam_ref, lam_init):
    lf = lam_ref[...]
    a = jnp.sum(lf[0:1] * lf[1:2], axis=-1, keepdims=True)
    b = jnp.sum(lf[2:3] * lf[3:4], axis=-1, keepdims=True)
    return jnp.exp(a) - jnp.exp(b) + lam_init


def _sub_norm(od, gsub, lam_init):
    return _rms(od, gsub) * (1.0 - lam_init)


def _softmax_step(s, vb, m, l, acc):
    m_new = jnp.maximum(m, jnp.max(s, axis=-1, keepdims=True))
    alpha = jnp.exp(m - m_new)
    p = jnp.exp(s - m_new)
    l = alpha * l + jnp.sum(p, axis=-1, keepdims=True)
    acc = alpha * acc + _dot(p.astype(BF16), vb)
    return m_new, l, acc


POS_SPLIT = 64


def _attn_kernel(slopes_ref, lam_ref, gsub_ref, q_ref, k_ref, vt_ref, pf_ref, o_ref, acc_ref, s0_ref, s1_ref, l_ref,
                 *, t, lam_init):
    h = pl.program_id(1)
    i = pl.program_id(2)
    slope = slopes_ref[h]
    q = q_ref[...]
    lane = lax.broadcasted_iota(I32, q.shape, 1)
    zero = jnp.zeros_like(q)
    feat = jnp.where(lane == 0, slope * POS_SPLIT, jnp.where(lane == 1, slope, 0.0)).astype(BF16)
    qa = (jnp.concatenate([jnp.where(lane < HEAD_DIM, q, zero), feat], axis=1),
          jnp.concatenate([jnp.where(lane >= HEAD_DIM, q, zero), feat], axis=1))
    acc_ref[...] = jnp.zeros(acc_ref.shape, F32)
    s_bufs = (s0_ref, s1_ref)

    def scores(j, buf):
        k0 = pl.multiple_of(j * t, t)
        kaug = jnp.concatenate([k_ref[pl.ds(k0, t), :], pf_ref[pl.ds(k0, t), :]], axis=1)
        for c in range(2):
            s_bufs[buf][c] = _dot_nt(kaug, qa[c])

    def absorb(j, buf, stats, masked):
        k0 = pl.multiple_of(j * t, t)
        vt = vt_ref[:, pl.ds(k0, t)]
        out = []
        for c in range(2):
            m, l = stats[c]
            s = s_bufs[buf][c]
            if masked:
                kr = lax.broadcasted_iota(I32, s.shape, 0)
                qc = lax.broadcasted_iota(I32, s.shape, 1)
                s = jnp.where(kr <= qc, s, NEG)
            m_new = jnp.maximum(m, jnp.max(s, axis=0, keepdims=True))
            alpha = jnp.exp(m - m_new)
            p = jnp.exp(s - m_new)
            l = alpha * l + jnp.sum(p, axis=0, keepdims=True)
            acc_ref[c] = alpha * acc_ref[c] + _dot(vt, p.astype(BF16))
            out.append((m_new, l))
        return tuple(out)

    def pair(jj, stats):
        j = 2 * jj
        scores(j + 1, 1)
        stats = absorb(j, 0, stats, False)
        scores(j + 2, 0)
        return absorb(j + 1, 1, stats, False)

    init = ((jnp.full((1, t), NEG, F32), jnp.zeros((1, t), F32)),) * 2
    scores(0, 0)
    stats = lax.fori_loop(0, i // 2, pair, init)

    def finish(final):
        for c in range(2):
            l_ref[c] = final[c][1]

    @pl.when(i % 2 == 0)
    def _():
        finish(absorb(i, 0, stats, True))

    @pl.when(i % 2 == 1)
    def _():
        scores(i, 1)
        finish(absorb(i, 1, absorb(i - 1, 0, stats, False), True))

    od = acc_ref[0] / l_ref[0] - _lambda_value(lam_ref, lam_init) * (acc_ref[1] / l_ref[1])
    y = od * lax.rsqrt(jnp.mean(od * od, axis=0, keepdims=True) + EPS) * gsub_ref[...] * (1.0 - lam_init)
    o_ref[...] = y.T.astype(o_ref.dtype)


def _attn(q, kb, vt, slopes, lam, gsub_col, seq, lam_init):
    n = q.shape[0]
    b = n // seq
    t = _tile(seq, 512)
    nq = seq // t
    assert seq <= POS_SPLIT * 256, "key positions must split into two bf16-exact factors"
    pos = jnp.arange(seq)
    pf = jnp.zeros((seq, V_DIM), F32).at[:, 0].set(pos // POS_SPLIT).at[:, 1].set(pos % POS_SPLIT).astype(BF16)
    kern = functools.partial(_attn_kernel, t=t, lam_init=lam_init)
    return pl.pallas_call(
        kern,
        grid=(b, N_HEADS, nq),
        in_specs=[pl.BlockSpec(memory_space=pltpu.SMEM),
                  pl.BlockSpec(lam.shape, lambda bi, h, i: (0, 0)),
                  pl.BlockSpec((V_DIM, 1), lambda bi, h, i: (0, 0)),
                  pl.BlockSpec((t, V_DIM), lambda bi, h, i: (bi * nq + i, h)),
                  pl.BlockSpec((seq, V_DIM), lambda bi, h, i: (bi, h)),
                  pl.BlockSpec((V_DIM, seq), lambda bi, h, i: (h, bi)),
                  pl.BlockSpec((seq, V_DIM), lambda bi, h, i: (0, 0))],
        out_specs=pl.BlockSpec((t, V_DIM), lambda bi, h, i: (bi * nq + i, h)),
        out_shape=jax.ShapeDtypeStruct((n, ATT_WIDTH), BF16),
        scratch_shapes=[pltpu.VMEM((2, V_DIM, t), F32),
                        pltpu.VMEM((2, t, t), F32),
                        pltpu.VMEM((2, t, t), F32),
                        pltpu.VMEM((2, 1, t), F32)],
        compiler_params=_params("parallel", "parallel", "arbitrary"),
        name="prompt_attn",
    )(slopes, lam, gsub_col, q, kb, vt, pf)


def _page_copies(pt_ref, ck_hbm, cv_hbm, kbuf, vbuf, sem, step, slot, *, layer, n_chunks, pages_per_chunk, page):
    b = step // n_chunks
    c = step % n_chunks
    n_pages = n_chunks * pages_per_chunk
    copies = []
    for p in range(pages_per_chunk):
        phys = pt_ref[b * n_pages + c * pages_per_chunk + p]
        dst = pl.ds(p * page, page)
        copies.append(pltpu.make_async_copy(ck_hbm.at[layer, phys], kbuf.at[slot, dst], sem.at[0, slot]))
        copies.append(pltpu.make_async_copy(cv_hbm.at[layer, phys], vbuf.at[slot, dst], sem.at[1, slot]))
    return copies


def _sample_attn_kernel(pt_ref, slopes_ref, lam_ref, gsub_ref, q_ref, kn_ref, vn_ref, ck_hbm, cv_hbm, o_ref,
                        kbuf, vbuf, sem, m_ref, l_ref, acc_ref, *, layer, n_chunks, pages_per_chunk, page, t_new,
                        lam_init):
    b = pl.program_id(0)
    c = pl.program_id(1)
    step = b * n_chunks + c
    n_steps = pl.num_programs(0) * n_chunks
    slot = step % 2
    copies = functools.partial(_page_copies, pt_ref, ck_hbm, cv_hbm, kbuf, vbuf, sem, layer=layer,
                               n_chunks=n_chunks, pages_per_chunk=pages_per_chunk, page=page)

    @pl.when(step == 0)
    def _():
        for cp in copies(step, slot):
            cp.start()

    @pl.when(step + 1 < n_steps)
    def _():
        for cp in copies(step + 1, 1 - slot):
            cp.start()

    @pl.when(c == 0)
    def _():
        m_ref[...] = jnp.full(m_ref.shape, NEG, F32)
        l_ref[...] = jnp.zeros(l_ref.shape, F32)
        acc_ref[...] = jnp.zeros(acc_ref.shape, F32)

    qh = q_ref[0]
    n_rows = qh.shape[0]
    ck = pages_per_chunk * page
    keys = ck // N_HEADS
    past = n_chunks * keys
    rows = lax.broadcasted_iota(I32, (n_rows, 1), 0)
    tok = rows % t_new
    head = rows // (2 * t_new)
    slope = jnp.zeros((n_rows, 1), F32)
    for hh in range(N_HEADS):
        slope = jnp.where(head == hh, slopes_ref[hh], slope)

    for cp in copies(step, slot):
        cp.wait()

    kb = kbuf[slot].astype(BF16)
    vb = vbuf[slot].astype(BF16)
    cols = lax.broadcasted_iota(I32, (n_rows, ck), 1)
    dist = (past + tok) - (c * keys + cols // N_HEADS)
    s = _dot_nt(qh, kb) - slope * dist.astype(F32)
    s = jnp.where(cols % N_HEADS == head, s, NEG)
    m, l, acc = _softmax_step(s, vb, m_ref[...], l_ref[...], acc_ref[...])
    m_ref[...] = m
    l_ref[...] = l
    acc_ref[...] = acc

    @pl.when(c == n_chunks - 1)
    def _():
        knb = kn_ref[0].astype(BF16)
        vnb = vn_ref[0].astype(BF16)
        ncol = lax.broadcasted_iota(I32, (n_rows, knb.shape[0]), 1)
        nkey = ncol // N_HEADS
        nd = tok - nkey
        sn = _dot_nt(qh, knb) - slope * nd.astype(F32)
        sn = jnp.where((nd >= 0) & (nkey < t_new) & (ncol % N_HEADS == head), sn, NEG)
        _, l2, acc2 = _softmax_step(sn, vnb, m, l, acc)
        o = acc2 / l2
        od = o - _lambda_value(lam_ref, lam_init) * pltpu.roll(o, n_rows - t_new, axis=0)
        o_ref[0] = _sub_norm(od, gsub_ref[...], lam_init).astype(o_ref.dtype)


def _sample_attn(q, kn, vn, cache_k, cache_v, layer, page_table, slopes, lam, gsub, lam_init):
    b, t_new, width = q.shape
    depth, n_phys, page = cache_k.shape[:3]
    n_pages = page_table.shape[1]
    ppc = 8 if n_pages % 8 == 0 else n_pages
    n_chunks = n_pages // ppc
    n_rows = N_HEADS * 2 * t_new
    pairs = page * N_HEADS
    qh = q.reshape(b, t_new, N_HEADS, V_DIM).transpose(0, 2, 1, 3)
    map_mask = (jnp.arange(V_DIM)[None, :] // HEAD_DIM == jnp.arange(2)[:, None]).astype(q.dtype)
    qh = (qh[:, :, None] * map_mask[None, None, :, None, :]).reshape(b, n_rows, V_DIM)
    new_rows = lambda a: jnp.pad(a, ((0, 0), (0, SUBLANES - t_new), (0, 0), (0, 0))).reshape(b, -1, V_DIM)
    kern = functools.partial(_sample_attn_kernel, layer=layer, n_chunks=n_chunks, pages_per_chunk=ppc, page=pairs,
                             t_new=t_new, lam_init=lam_init)
    grid_spec = pltpu.PrefetchScalarGridSpec(
        num_scalar_prefetch=1,
        grid=(b, n_chunks),
        in_specs=[pl.BlockSpec(memory_space=pltpu.SMEM),
                  pl.BlockSpec(lam.shape, lambda bi, c, pt: (0, 0)),
                  pl.BlockSpec((1, V_DIM), lambda bi, c, pt: (0, 0)),
                  pl.BlockSpec((1, n_rows, V_DIM), lambda bi, c, pt: (bi, 0, 0)),
                  pl.BlockSpec((1, SUBLANES * N_HEADS, V_DIM), lambda bi, c, pt: (bi, 0, 0)),
                  pl.BlockSpec((1, SUBLANES * N_HEADS, V_DIM), lambda bi, c, pt: (bi, 0, 0)),
                  pl.BlockSpec(memory_space=pl.ANY),
                  pl.BlockSpec(memory_space=pl.ANY)],
        out_specs=pl.BlockSpec((1, n_rows, V_DIM), lambda bi, c, pt: (bi, 0, 0)),
        scratch_shapes=[pltpu.VMEM((2, ppc * pairs, V_DIM), F32),
                        pltpu.VMEM((2, ppc * pairs, V_DIM), F32),
                        pltpu.SemaphoreType.DMA((2, 2)),
                        pltpu.VMEM((n_rows, 1), F32),
                        pltpu.VMEM((n_rows, 1), F32),
                        pltpu.VMEM((n_rows, V_DIM), F32)])
    out = pl.pallas_call(
        kern,
        grid_spec=grid_spec,
        out_shape=jax.ShapeDtypeStruct((b, n_rows, V_DIM), BF16),
        compiler_params=_params("arbitrary", "arbitrary"),
        name="sample_attn",
    )(page_table.reshape(-1), slopes, lam, gsub, qh, new_rows(kn), new_rows(vn),
      cache_k.reshape(depth, n_phys, pairs, V_DIM), cache_v.reshape(depth, n_phys, pairs, V_DIM))
    out = out.reshape(b, N_HEADS, 2, t_new, V_DIM)[:, :, 0]
    return out.transpose(0, 2, 1, 3).reshape(b, t_new, width)


def _outproj_kernel(x_ref, oatt_ref, up_ref, halo_ref, wbd_ref, pscale_ref, wout_ref, o_ref, ext_ref, *,
                    tm, n_prev, zero_first_halo):
    i = pl.program_id(1)
    halo = halo_ref[0]
    if zero_first_halo:
        halo = jnp.where(i == 0, 0.0, halo)
    cur = up_ref[0]
    ext_ref[0:HALO, :] = halo
    ext_ref[HALO:HALO + tm, :] = cur
    gw = cur.shape[1] // len(POOL_WINDOWS)
    pos1 = lax.broadcasted_iota(I32, (tm, gw), 0) + (i * tm + n_prev + 1)
    ds = []
    for g, w in enumerate(POOL_WINDOWS):
        cs = slice(g * gw, (g + 1) * gw)
        acc = cur[:, cs]
        for sft in range(1, w):
            acc = acc + ext_ref[HALO - sft:HALO - sft + tm, cs]
        cnt = jnp.minimum(pos1, w).astype(F32)
        ds.append(acc / cnt - cur[:, cs])
    d = jnp.concatenate(ds, axis=1)
    o_pool = _dot(d.astype(BF16), wbd_ref[...]) * pscale_ref[...]
    aw = oatt_ref.shape[2]
    o_ref[0] = (x_ref[0] + _dot(oatt_ref[0], wout_ref[0:aw, :])
                + _dot(o_pool.astype(BF16), wout_ref[aw:, :]))


def _outproj(x, o_att, up, halo_arr, wbd, pool_scale, w_out_b, *, n_prev, halo_from_up):
    b, t, d = x.shape
    aw = o_att.shape[2]
    pw = up.shape[2]
    tm = _tile(t, 512)
    if halo_from_up:
        halo_map = lambda bi, i: (bi, jnp.maximum(i * (tm // HALO) - 1, 0), 0)
    else:
        halo_map = lambda bi, i: (bi, 0, 0)
    kern = functools.partial(_outproj_kernel, tm=tm, n_prev=n_prev, zero_first_halo=halo_from_up)
    full = lambda shape: pl.BlockSpec(shape, lambda bi, i: (0,) * len(shape))
    return pl.pallas_call(
        kern,
        grid=(b, t // tm),
        in_specs=[pl.BlockSpec((1, tm, d), lambda bi, i: (bi, i, 0)),
                  pl.BlockSpec((1, tm, aw), lambda bi, i: (bi, i, 0)),
                  pl.BlockSpec((1, tm, pw), lambda bi, i: (bi, i, 0)),
                  pl.BlockSpec((1, HALO, pw), halo_map),
                  full((pw, pw)), full((1, pw)), full((aw + pw, d))],
        out_specs=pl.BlockSpec((1, tm, d), lambda bi, i: (bi, i, 0)),
        out_shape=jax.ShapeDtypeStruct((b, t, d), F32),
        scratch_shapes=[pltpu.VMEM((HALO + tm, pw), F32)],
        compiler_params=_params("parallel", "arbitrary"),
        name="pool_out_proj",
    )(x, o_att, up, halo_arr, wbd, pool_scale, w_out_b)


def _memkv_kernel(mem_ref, gmem_ref, w_ref, gxk_ref, gmat_ref, mk_ref, mv_ref):
    xw = mk_ref.shape[1]
    kv = _dot(_rms(mem_ref[...], gmem_ref[...]).astype(BF16), w_ref[...])
    kx = kv[:, :xw]
    mk_ref[...] = kx * lax.rsqrt(_group_meansq(kx, gmat_ref[...]) + EPS) * gxk_ref[...]
    mv_ref[...] = kv[:, xw:]


def _memkv(mem2d, g_mem, w_xkv_b, gxk, gmat):
    n, d = mem2d.shape
    xw = w_xkv_b.shape[1] // 2
    tm = _tile(n, 512)
    full = lambda shape: pl.BlockSpec(shape, lambda i: (0,) * len(shape))
    return pl.pallas_call(
        _memkv_kernel,
        grid=(n // tm,),
        in_specs=[pl.BlockSpec((tm, d), lambda i: (i, 0)), full((1, d)), full((d, 2 * xw)),
                  full((1, xw)), full((MXU_DIM, MXU_DIM))],
        out_specs=[pl.BlockSpec((tm, xw), lambda i: (i, 0))] * 2,
        out_shape=[jax.ShapeDtypeStruct((n, xw), F32)] * 2,
        compiler_params=_params("parallel"),
        name="mem_kv",
    )(mem2d, g_mem, w_xkv_b, gxk, gmat)


def _xattn_kernel(h_ref, mk_ref, mv_ref, gx_ref, wq_ref, gxq_ref, gmat_ref, wo_ref, o_ref):
    h1 = h_ref[0]
    qx = _dot(_rms(h1, gx_ref[...]).astype(BF16), wq_ref[...])
    qn = (qx * lax.rsqrt(_group_meansq(qx, gmat_ref[...]) + EPS) * gxq_ref[...]).astype(BF16)
    mk = mk_ref[0].astype(BF16)
    mv = mv_ref[0].astype(BF16)
    head_of_lane = lax.broadcasted_iota(I32, qn.shape, 1) // X_HEAD_DIM
    o = jnp.zeros(qn.shape, F32)
    for hh in range(X_HEADS):
        sel = head_of_lane == hh
        s = _dot_nt(jnp.where(sel, qn, jnp.zeros_like(qn)), mk)
        p = jnp.exp(s - jnp.max(s, axis=-1, keepdims=True))
        p = p / jnp.sum(p, axis=-1, keepdims=True)
        o = o + jnp.where(sel, _dot(p.astype(BF16), mv), 0.0)
    o_ref[0] = h1 + _dot(o.astype(BF16), wo_ref[...])


def _xattn(h1, mk, mv, g_xattn, w_xq_b, gxq, gmat, w_xo_b):
    b, t, d = h1.shape
    nm, xw = mk.shape[1], mk.shape[2]
    tq = _tile(t, 512)
    full = lambda shape: pl.BlockSpec(shape, lambda bi, i: (0,) * len(shape))
    return pl.pallas_call(
        _xattn_kernel,
        grid=(b, t // tq),
        in_specs=[pl.BlockSpec((1, tq, d), lambda bi, i: (bi, i, 0)),
                  pl.BlockSpec((1, nm, xw), lambda bi, i: (bi, 0, 0)),
                  pl.BlockSpec((1, nm, xw), lambda bi, i: (bi, 0, 0)),
                  full((1, d)), full((d, xw)), full((1, xw)), full((MXU_DIM, MXU_DIM)), full((xw, d))],
        out_specs=pl.BlockSpec((1, tq, d), lambda bi, i: (bi, i, 0)),
        out_shape=jax.ShapeDtypeStruct((b, t, d), F32),
        compiler_params=_params("parallel", "parallel"),
        name="mem_xattn",
    )(h1, mk, mv, g_xattn, w_xq_b, gxq, gmat, w_xo_b)


def _gather_rows(t, n_experts):
    return -(-(TOP_K * t + n_experts * (SUBLANES - 1)) // LANES) * LANES


def _slot_onehot(pos, n_rows):
    t = pos.shape[0]
    r = lax.broadcasted_iota(I32, (t, n_rows), 1)
    hit = r == pos[:, 0:1]
    for k in range(1, TOP_K):
        hit = hit | (r == pos[:, k:k + 1])
    return jnp.where(hit, 1.0, 0.0).astype(BF16)


def _route_kernel(h_ref, gffn_ref, wr_ref, br_ref, xg_ref, pos_ref, cnt_ref, *, n_experts):
    t = h_ref.shape[0]
    n_rows = xg_ref.shape[1]
    mb = _rms(h_ref[...], gffn_ref[...]).astype(BF16)
    logits = _dot(mb, wr_ref[...]) + br_ref[...]
    lane = lax.broadcasted_iota(I32, logits.shape, 1).astype(F32)
    work = logits
    sels, vals = [], []
    for _ in range(TOP_K):
        mx = jnp.max(work, axis=-1, keepdims=True)
        idx = jnp.min(jnp.where(work == mx, lane, float(LANES)), axis=-1, keepdims=True)
        sel = lane == idx
        sels.append(sel)
        vals.append(mx)
        work = jnp.where(sel, 2.0 * NEG, work)
    es = [jnp.exp(v - vals[0]) for v in vals]
    denom = es[0]
    for e in es[1:]:
        denom = denom + e
    hot = jnp.zeros(logits.shape, F32)
    gate_m = jnp.zeros(logits.shape, F32)
    for sel, e in zip(sels, es):
        hot = jnp.where(sel, 1.0, hot)
        gate_m = jnp.where(sel, e / denom, gate_m)
    ri = lax.broadcasted_iota(I32, (t, t), 0)
    ci = lax.broadcasted_iota(I32, (t, t), 1)
    rank = _dot(jnp.where(ci < ri, 1.0, 0.0).astype(BF16), hot.astype(BF16))
    count = jnp.sum(hot, axis=0, keepdims=True).astype(I32)
    c8 = (((count + (SUBLANES - 1)) >> 3) << 3).astype(F32)
    ei = lax.broadcasted_iota(I32, (LANES, LANES), 0)
    ej = lax.broadcasted_iota(I32, (LANES, LANES), 1)
    off = _dot(jnp.broadcast_to(c8, (SUBLANES, LANES)).astype(BF16), jnp.where(ei < ej, 1.0, 0.0).astype(BF16))[0:1]
    slot = off + rank
    lane_i = lax.broadcasted_iota(I32, logits.shape, 1)
    pos = jnp.full(logits.shape, -1, I32)
    for k, sel in enumerate(sels):
        pk = jnp.sum(jnp.where(sel, slot, 0.0), axis=-1, keepdims=True).astype(I32)
        pos = jnp.where(lane_i == k, pk, pos)
    pos_ref[...] = pos
    cnt_ref[0] = jnp.broadcast_to(count, (SUBLANES, LANES))
    g1 = gate_m.astype(BF16)
    r1 = gate_m - g1.astype(F32)
    g2 = r1.astype(BF16)
    g3 = (r1 - g2.astype(F32)).astype(BF16)
    gx = (g1.astype(F32) + pltpu.roll(g2.astype(F32), n_experts, axis=1)
          + pltpu.roll(g3.astype(F32), 2 * n_experts, axis=1)).astype(BF16)
    xg_ref[0] = _dot_tn(_slot_onehot(pos, n_rows), jnp.concatenate([mb, gx], axis=1))


def _route(h2d, g_ffn, wr_pad_b, br_pad, n_experts):
    n, d = h2d.shape
    t = _tile(n, 256)
    n_sub = n // t
    n_rows = _gather_rows(t, n_experts)
    kern = functools.partial(_route_kernel, n_experts=n_experts)
    full = lambda shape: pl.BlockSpec(shape, lambda i: (0,) * len(shape))
    return pl.pallas_call(
        kern,
        grid=(n_sub,),
        in_specs=[pl.BlockSpec((t, d), lambda i: (i, 0)), full((1, d)), full((d, LANES)), full((1, LANES))],
        out_specs=[pl.BlockSpec((1, n_rows, d + LANES), lambda i: (i, 0, 0)),
                   pl.BlockSpec((t, LANES), lambda i: (i, 0)),
                   pl.BlockSpec((1, SUBLANES, LANES), lambda i: (i, 0, 0))],
        out_shape=[jax.ShapeDtypeStruct((n_sub, n_rows, d + LANES), F32),
                   jax.ShapeDtypeStruct((n, LANES), I32),
                   jax.ShapeDtypeStruct((n_sub, SUBLANES, LANES), I32)],
        compiler_params=_params("parallel"),
        name="moe_route_gather",
    )(h2d, g_ffn, wr_pad_b, br_pad)


def _unit_plan(cnt, n_rows, n_experts):
    n_sub = cnt.shape[0]
    upt = n_rows // SUBLANES
    u = (cnt + (SUBLANES - 1)) // SUBLANES
    uoff = jnp.cumsum(u, axis=1) - u
    used = jnp.sum(u, axis=1, keepdims=True)
    u_all = jnp.concatenate([u, upt - used], axis=1)
    uoff_all = jnp.concatenate([uoff, used], axis=1)
    flat_cnt = u_all.T.reshape(-1)
    cum = jnp.cumsum(flat_cnt)
    start = cum - flat_cnt
    k = jnp.arange(upt, dtype=I32)
    grp = jnp.sum(uoff_all[:, None, 1:] <= k[None, :, None], axis=-1)
    shift = start.reshape(n_experts + 1, n_sub).T - uoff_all
    hit = grp[:, :, None] == jnp.arange(n_experts + 1, dtype=I32)[None, None, :]
    dest = k[None, :] + jnp.sum(jnp.where(hit, shift[:, None, :], 0), axis=-1)
    n_units = n_sub * upt
    unit_src = jnp.zeros((n_units,), I32).at[dest.reshape(-1)].set(jnp.arange(n_units, dtype=I32))
    ustart = jnp.concatenate([start[::n_sub], cum[-1:]]).astype(I32)
    return ustart, unit_src


def _expert_kernel(ustart_ref, usrc_ref, xg_hbm, wgu_ref, bgu_ref, wd_ref, bd_ref, yg_hbm,
                   xbuf, ybuf, wgu_b, wd_b, in_sem, out_sem, *, n_experts, upb, d_model):
    e = pl.program_id(0)
    u0 = ustart_ref[e]
    u1 = ustart_ref[e + 1]
    dff = wd_b.shape[0]

    def in_copy(j, slot, i):
        return pltpu.make_async_copy(xg_hbm.at[usrc_ref[j]], xbuf.at[slot, pl.ds(i * SUBLANES, SUBLANES)],
                                     in_sem.at[slot])

    def out_copy(j, i):
        return pltpu.make_async_copy(ybuf.at[pl.ds(i * SUBLANES, SUBLANES)], yg_hbm.at[usrc_ref[j]], out_sem.at[0])

    def n_valid(blk):
        return jnp.minimum(upb, u1 - (u0 + blk * upb))

    def for_units(blk, fn):
        base = u0 + blk * upb
        nv = n_valid(blk)

        @pl.when(nv == upb)
        def _():
            for i in range(upb):
                fn(base + i, i)

        @pl.when(nv < upb)
        def _():
            def body(i, c):
                fn(base + i, i)
                return c
            lax.fori_loop(0, nv, body, 0)

    @pl.when(e == 0)
    def _():
        xbuf[...] = jnp.zeros(xbuf.shape, F32)

    @pl.when(e < n_experts)
    def _():
        rc = 128
        for r in range(0, wgu_b.shape[0], rc):
            wgu_b[r:r + rc, :] = wgu_ref[0, r:r + rc, :].astype(BF16)
        for r in range(0, wd_b.shape[0], rc):
            wd_b[r:r + rc, :] = wd_ref[0, r:r + rc, :].astype(BF16)
        n_blk = (u1 - u0 + upb - 1) // upb

        @pl.when(n_blk > 0)
        def _():
            for_units(0, lambda j, i: in_copy(j, 0, i).start())

        def block(blk, c):
            slot = blk % 2

            @pl.when(blk + 1 < n_blk)
            def _():
                for_units(blk + 1, lambda j, i: in_copy(j, 1 - slot, i).start())

            for_units(blk, lambda j, i: in_copy(j, slot, i).wait())
            x = xbuf[slot]
            gcols = x[:, d_model:]
            lane = lax.broadcasted_iota(I32, gcols.shape, 1)
            own = (lane % n_experts == e) & (lane < 3 * n_experts)
            gate_w = jnp.sum(jnp.where(own, gcols, 0.0), axis=-1, keepdims=True)
            xb = x[:, :d_model].astype(BF16)
            hidden = []
            for c0 in range(0, dff, MXU_DIM):
                gs = slice(c0, c0 + MXU_DIM)
                us = slice(dff + c0, dff + c0 + MXU_DIM)
                gate = jnp.minimum(_dot(xb, wgu_b[:, gs]) + bgu_ref[0, :, gs], SWIGLU_LIMIT)
                up = jnp.clip(_dot(xb, wgu_b[:, us]) + bgu_ref[0, :, us], -SWIGLU_LIMIT, SWIGLU_LIMIT)
                glu = gate * (1.0 / (1.0 + jnp.exp(-SWIGLU_ALPHA * gate)))
                hidden.append(((up + 1.0) * glu).astype(BF16))
            y = (_dot(jnp.concatenate(hidden, axis=1), wd_b[...]) + bd_ref[0]) * gate_w

            @pl.when(blk > 0)
            def _():
                for_units(blk - 1, lambda j, i: out_copy(j, i).wait())

            ybuf[...] = y
            for_units(blk, lambda j, i: out_copy(j, i).start())
            return c

        lax.fori_loop(0, n_blk, block, 0)

        @pl.when(n_blk > 0)
        def _():
            for_units(n_blk - 1, lambda j, i: out_copy(j, i).wait())

    @pl.when(e == n_experts)
    def _():
        ybuf[...] = jnp.zeros(ybuf.shape, F32)

        def start(j, c):
            out_copy(j, 0).start()
            return c

        def wait(j, c):
            out_copy(j, 0).wait()
            return c
        lax.fori_loop(u0, u1, start, 0)
        lax.fori_loop(u0, u1, wait, 0)


def _experts(ustart, unit_src, xg, w_gu, b_gu, w_d, b_d, tr):
    n_sub, n_rows, dx = xg.shape
    n_experts, d_model, dff2 = w_gu.shape
    dff = w_d.shape[1]
    n_units = n_sub * n_rows // SUBLANES
    upb = tr // SUBLANES
    kern = functools.partial(_expert_kernel, n_experts=n_experts, upb=upb, d_model=d_model)
    last = n_experts - 1
    grid_spec = pltpu.PrefetchScalarGridSpec(
        num_scalar_prefetch=2,
        grid=(n_experts + 1,),
        in_specs=[pl.BlockSpec(memory_space=pl.ANY),
                  pl.BlockSpec((1, d_model, dff2), lambda e, us, src: (jnp.minimum(e, last), 0, 0)),
                  pl.BlockSpec((1, 1, dff2), lambda e, us, src: (jnp.minimum(e, last), 0, 0)),
                  pl.BlockSpec((1, dff, d_model), lambda e, us, src: (jnp.minimum(e, last), 0, 0)),
                  pl.BlockSpec((1, 1, d_model), lambda e, us, src: (jnp.minimum(e, last), 0, 0))],
        out_specs=pl.BlockSpec(memory_space=pl.ANY),
        scratch_shapes=[pltpu.VMEM((2, tr, dx), F32),
                        pltpu.VMEM((tr, d_model), F32),
                        pltpu.VMEM((d_model, dff2), BF16),
                        pltpu.VMEM((dff, d_model), BF16),
                        pltpu.SemaphoreType.DMA((2,)),
                        pltpu.SemaphoreType.DMA((1,))])
    yg = pl.pallas_call(
        kern,
        grid_spec=grid_spec,
        out_shape=jax.ShapeDtypeStruct((n_units, SUBLANES, d_model), F32),
        compiler_params=_params("arbitrary"),
        name="moe_experts",
    )(ustart, unit_src, xg.reshape(n_units, SUBLANES, dx), w_gu, b_gu.reshape(n_experts, 1, dff2),
      w_d, b_d.reshape(n_experts, 1, d_model))
    return yg.reshape(n_sub, n_rows, d_model)


def _combine_kernel(yg_ref, pos_ref, h_ref, o_ref):
    onehot = _slot_onehot(pos_ref[...], yg_ref.shape[1])
    o_ref[...] = h_ref[...] + _dot(onehot, yg_ref[0].astype(BF16))


def _combine(yg, pos, h2d):
    n_sub, n_rows, d = yg.shape
    n = h2d.shape[0]
    t = n // n_sub
    return pl.pallas_call(
        _combine_kernel,
        grid=(n_sub,),
        in_specs=[pl.BlockSpec((1, n_rows, d), lambda i: (i, 0, 0)),
                  pl.BlockSpec((t, LANES), lambda i: (i, 0)),
                  pl.BlockSpec((t, d), lambda i: (i, 0))],
        out_specs=pl.BlockSpec((t, d), lambda i: (i, 0)),
        out_shape=jax.ShapeDtypeStruct((n, d), F32),
        compiler_params=_params("parallel"),
        name="moe_combine",
    )(yg, pos, h2d)


def _moe(h, g_ffn, wr_pad_b, br_pad, w_gu, b_gu, w_d, b_d):
    b, t, d = h.shape
    n_experts = w_gu.shape[0]
    h2d = h.reshape(b * t, d)
    xg, pos, cnt = _route(h2d, g_ffn, wr_pad_b, br_pad, n_experts)
    ustart, unit_src = _unit_plan(cnt[:, 0, :n_experts], xg.shape[1], n_experts)
    tr = 512 if b * t >= 4096 else 128
    yg = _experts(ustart, unit_src, xg, w_gu, b_gu, w_d, b_d, tr)
    return _combine(yg, pos, h2d).reshape(b, t, d)


def kernel(x_prompt, x_sample, mem_prompt, cache_k, cache_v, cache_mem_k, cache_mem_v, state_pool, page_table,
           g_mix, w_in, g_q, g_k, lam, g_sub, w_pool, pool_scale, w_out,
           g_xattn, g_mem, w_xq, w_xkv, g_xq, g_xk, w_xo,
           g_ffn, w_router, b_router, w_gate_up, b_gate_up, w_down, b_down):
    depth = w_in.shape[0]
    b_p, s_p, d = x_prompt.shape
    b_s, t_s, _ = x_sample.shape
    n_mem = mem_prompt.shape[1]
    n_experts = w_router.shape[2]
    assert LANES % n_experts == 0 and 3 * n_experts <= LANES
    assert ATT_WIDTH % MXU_DIM == 0 and (X_HEADS * X_HEAD_DIM) % MXU_DIM == 0

    slopes = jnp.asarray([2.0 ** (-8.0 * (h + 1) / N_HEADS) for h in range(N_HEADS)], F32)
    grp = jnp.arange(MXU_DIM) // HEAD_DIM
    gmat = (grp[:, None] == grp[None, :]).astype(BF16)
    row = lambda a: a.reshape(1, -1).astype(F32)

    hp, hs = x_prompt, x_sample
    outs = [[] for _ in range(8)]
    for l in range(depth):
        lam_init = 0.8 - 0.6 * math.exp(-0.3 * l)
        w_in_b = w_in[l].astype(BF16)
        w_out_b = w_out[l].astype(BF16)
        w_xq_b = w_xq[l].astype(BF16)
        w_xkv_b = w_xkv[l].astype(BF16)
        w_xo_b = w_xo[l].astype(BF16)
        gqk = jnp.concatenate([jnp.tile(g_q[l].reshape(-1), N_HEADS), jnp.tile(g_k[l].reshape(-1), N_HEADS)])[None, :]
        gxq = (jnp.tile(g_xq[l], X_HEADS) * (X_HEAD_DIM ** -0.5))[None, :]
        gxk = jnp.tile(g_xk[l], X_HEADS)[None, :]
        wbd = jax.scipy.linalg.block_diag(*[w_pool[l, g] for g in range(len(POOL_WINDOWS))]).astype(BF16)
        wr_pad_b = jnp.pad(w_router[l], ((0, 0), (0, LANES - n_experts))).astype(BF16)
        br_pad = jnp.pad(b_router[l].astype(F32), (0, LANES - n_experts), constant_values=NEG)[None, :]
        moe_w = (row(g_ffn[l]), wr_pad_b, br_pad, w_gate_up[l], b_gate_up[l], w_down[l], b_down[l])
        gsub = row(g_sub[l])

        q_b, k_b, v_t, k32, v32, up = _proj(hp.reshape(b_p * s_p, d), row(g_mix[l]), w_in_b, gqk, gmat)
        o_att = _attn(q_b, k_b, v_t, slopes, lam[l], g_sub[l].reshape(-1, 1).astype(F32), s_p, lam_init)
        up3 = up.reshape(b_p, s_p, -1)
        h1 = _outproj(hp, o_att.reshape(b_p, s_p, -1), up3, up3, wbd, row(pool_scale[l]), w_out_b, n_prev=0,
                      halo_from_up=True)
        mk, mv = _memkv(mem_prompt.reshape(b_p * n_mem, d), row(g_mem[l]), w_xkv_b, gxk, gmat)
        h2 = _xattn(h1, mk.reshape(b_p, n_mem, -1), mv.reshape(b_p, n_mem, -1), row(g_xattn[l]), w_xq_b, gxq, gmat,
                    w_xo_b)
        hp = _moe(h2, *moe_w)
        outs[0].append(k32.reshape(b_p, s_p, N_HEADS, V_DIM))
        outs[1].append(v32.reshape(b_p, s_p, N_HEADS, V_DIM))
        outs[2].append(up3[:, s_p - POOL_BUF:])
        outs[3].append(mk.reshape(b_p, n_mem, X_HEADS, X_HEAD_DIM))
        outs[4].append(mv.reshape(b_p, n_mem, X_HEADS, X_HEAD_DIM))

        q_s, _, _, k32_s, v32_s, up_s = _proj(hs.reshape(b_s * t_s, d), row(g_mix[l]), w_in_b, gqk, gmat)
        kn = k32_s.reshape(b_s, t_s, N_HEADS, V_DIM)
        vn = v32_s.reshape(b_s, t_s, N_HEADS, V_DIM)
        o_att_s = _sample_attn(q_s.reshape(b_s, t_s, -1), kn, vn, cache_k, cache_v, l, page_table, slopes, lam[l],
                               gsub, lam_init)
        ups = up_s.reshape(b_s, t_s, -1)
        halo_s = jnp.pad(state_pool[l], ((0, 0), (HALO - POOL_BUF, 0), (0, 0)))
        h1s = _outproj(hs, o_att_s, ups, halo_s, wbd, row(pool_scale[l]), w_out_b, n_prev=POOL_BUF, halo_from_up=False)
        h2s = _xattn(h1s, cache_mem_k[l].reshape(b_s, n_mem, -1), cache_mem_v[l].reshape(b_s, n_mem, -1),
                     row(g_xattn[l]), w_xq_b, gxq, gmat, w_xo_b)
        hs = _moe(h2s, *moe_w)
        outs[5].append(kn.reshape(b_s, t_s, N_HEADS, V_DIM))
        outs[6].append(vn.reshape(b_s, t_s, N_HEADS, V_DIM))
        outs[7].append(jnp.concatenate([state_pool[l], ups], axis=1)[:, t_s:])

    stacked = [jnp.stack(o) for o in outs]
    return (hp, hs, *stacked)
```

```python
import functools
import math

import jax
import jax.numpy as jnp
from jax import lax
from jax.experimental import pallas as pl
from jax.experimental.pallas import tpu as pltpu

F32 = jnp.float32
BF16 = jnp.bfloat16
I32 = jnp.int32

EPS = 1e-6
N_HEADS = 4
HEAD_DIM = 64
V_DIM = 2 * HEAD_DIM
ATT_WIDTH = N_HEADS * V_DIM
POOL_WINDOWS = (2, 4, 8, 16)
POOL_BUF = max(POOL_WINDOWS) - 1
HALO = POOL_BUF + 1
X_HEADS = 4
X_HEAD_DIM = 64
TOP_K = 4
SWIGLU_LIMIT = 7.0
SWIGLU_ALPHA = 1.702
NEG = -1e30

LANES = 128
SUBLANES = 8
MXU_DIM = 256
VMEM_LIMIT = 56 * 1024 * 1024


def _params(*sem):
    return pltpu.CompilerParams(dimension_semantics=sem, vmem_limit_bytes=VMEM_LIMIT)


def _tile(n, pref):
    return pref if n % pref == 0 else n


def _rms(x, g):
    return x * lax.rsqrt(jnp.mean(x * x, axis=-1, keepdims=True) + EPS) * g


def _dot(a, b):
    return jnp.dot(a, b, preferred_element_type=F32)


def _dot_nt(a, b):
    return lax.dot_general(a, b, (((1,), (1,)), ((), ())), preferred_element_type=F32)


def _dot_tn(a, b):
    return lax.dot_general(a, b, (((0,), (0,)), ((), ())), preferred_element_type=F32)


def _group_meansq(u, gmat):
    sq = u * u
    hi = sq.astype(BF16)
    lo = (sq - hi.astype(F32)).astype(BF16)
    outs = []
    for c in range(u.shape[1] // MXU_DIM):
        sl = slice(c * MXU_DIM, (c + 1) * MXU_DIM)
        outs.append(_dot(hi[:, sl], gmat) + _dot(lo[:, sl], gmat))
    return jnp.concatenate(outs, axis=1) * (1.0 / HEAD_DIM)


def _proj_kernel(x_ref, gmix_ref, w_ref, gqk_ref, gmat_ref, q_ref, kb_ref, vt_ref, k_ref, v_ref, up_ref):
    aw = ATT_WIDTH
    xn = _rms(x_ref[...], gmix_ref[...])
    u = _dot(xn.astype(BF16), w_ref[...])
    qk = u[:, :2 * aw]
    qkn = qk * lax.rsqrt(_group_meansq(qk, gmat_ref[...]) + EPS) * gqk_ref[...]
    kn = qkn[:, aw:]
    v = u[:, 2 * aw:3 * aw]
    for h in range(N_HEADS):
        hs = slice(h * V_DIM, (h + 1) * V_DIM)
        k_ref[:, h, :] = kn[:, hs]
        v_ref[:, h, :] = v[:, hs]
    up_ref[...] = u[:, 3 * aw:]
    q_ref[...] = (qkn[:, :aw] * (HEAD_DIM ** -0.5)).astype(BF16)
    kb_ref[...] = kn.astype(BF16)
    vt_ref[...] = v.T.astype(BF16)


def _proj(x2d, g_mix, w_in_b, gqk, gmat):
    n, d = x2d.shape
    wtot = w_in_b.shape[1]
    aw = ATT_WIDTH
    tm = _tile(n, 512)
    full = lambda shape: pl.BlockSpec(shape, lambda i: (0,) * len(shape))
    rows = lambda w: pl.BlockSpec((tm, w), lambda i: (i, 0))
    heads = pl.BlockSpec((tm, N_HEADS, V_DIM), lambda i: (i, 0, 0))
    return pl.pallas_call(
        _proj_kernel,
        grid=(n // tm,),
        in_specs=[rows(d), full((1, d)), full((d, wtot)), full((1, 2 * aw)), full((MXU_DIM, MXU_DIM))],
        out_specs=[rows(aw), rows(aw), pl.BlockSpec((aw, tm), lambda i: (0, i)), heads, heads, rows(wtot - 3 * aw)],
        out_shape=[jax.ShapeDtypeStruct((n, aw), BF16),
                   jax.ShapeDtypeStruct((n, aw), BF16),
                   jax.ShapeDtypeStruct((aw, n), BF16),
                   jax.ShapeDtypeStruct((n, N_HEADS, V_DIM), F32),
                   jax.ShapeDtypeStruct((n, N_HEADS, V_DIM), F32),
                   jax.ShapeDtypeStruct((n, wtot - 3 * aw), F32)],
        compiler_params=_params("parallel"),
        name="in_proj",
    )(x2d, g_mix, w_in_b, gqk, gmat)


def _lambda_value(lam_ref, lam_init):
    lf = lam_ref[...]
    a = jnp.sum(lf[0:1] * lf[1:2], axis=-1, keepdims=True)
    b = jnp.sum(lf[2:3] * lf[3:4], axis=-1, keepdims=True)
    return jnp.exp(a) - jnp.exp(b) + lam_init


def _sub_norm(od, gsub, lam_init):
    return _rms(od, gsub) * (1.0 - lam_init)


def _softmax_step(s, vb, m, l, acc):
    m_new = jnp.maximum(m, jnp.max(s, axis=-1, keepdims=True))
    alpha = jnp.exp(m - m_new)
    p = jnp.exp(s - m_new)
    l = alpha * l + jnp.sum(p, axis=-1, keepdims=True)
    acc = alpha * acc + _dot(p.astype(BF16), vb)
    return m_new, l, acc


POS_SPLIT = 64
ONES_ROWS = 2 * SUBLANES


def _attn_kernel(slopes_ref, lam_ref, gsub_ref, q_ref, k_ref, vt_ref, pf_ref, o_ref, acc_ref, s0_ref, s1_ref,
                 p0_ref, p1_ref, *, t, lam_init):
    h = pl.program_id(1)
    i = pl.program_id(2)
    slope = slopes_ref[h]
    q = q_ref[...]
    lane = lax.broadcasted_iota(I32, q.shape, 1)
    zero = jnp.zeros_like(q)
    feat = jnp.where(lane == 0, slope * POS_SPLIT, jnp.where(lane == 1, slope, 0.0)).astype(BF16)
    qa = (jnp.concatenate([jnp.where(lane < HEAD_DIM, q, zero), feat], axis=1),
          jnp.concatenate([jnp.where(lane >= HEAD_DIM, q, zero), feat], axis=1))
    acc_ref[...] = jnp.zeros(acc_ref.shape, F32)
    s_bufs = (s0_ref, s1_ref)

    def scores(j, buf):
        k0 = pl.multiple_of(j * t, t)
        kaug = jnp.concatenate([k_ref[pl.ds(k0, t), :], pf_ref[pl.ds(k0, t), :]], axis=1)
        for c in range(2):
            s_bufs[buf][c] = _dot_nt(kaug, qa[c])

    p_bufs = (p0_ref, p1_ref)

    def soft(buf, ms, masked):
        new_ms, alphas = [], []
        for c in range(2):
            s = s_bufs[buf][c]
            if masked:
                kr = lax.broadcasted_iota(I32, s.shape, 0)
                qc = lax.broadcasted_iota(I32, s.shape, 1)
                s = jnp.where(kr <= qc, s, NEG)
            m_new = jnp.maximum(ms[c], jnp.max(s, axis=0, keepdims=True))
            alphas.append(jnp.exp(ms[c] - m_new))
            p_bufs[buf][c] = jnp.exp(s - m_new).astype(BF16)
            new_ms.append(m_new)
        return tuple(new_ms), tuple(alphas)

    def accum(j, buf, alphas):
        k0 = pl.multiple_of(j * t, t)
        vt = jnp.concatenate([vt_ref[:, pl.ds(k0, t)], jnp.ones((ONES_ROWS, t), BF16)], axis=0)
        for c in range(2):
            acc_ref[c] = alphas[c] * acc_ref[c] + _dot(vt, p_bufs[buf][c])

    def pair(jj, carry):
        ms, pending = carry
        j = 2 * jj
        scores(j + 1, 1)
        ms, a0 = soft(0, ms, False)
        accum(jnp.maximum(j - 1, 0), 1, pending)
        scores(j + 2, 0)
        ms, a1 = soft(1, ms, False)
        accum(j, 0, a0)
        return ms, a1

    p1_ref[...] = jnp.zeros(p1_ref.shape, BF16)
    scores(0, 0)
    start = ((jnp.full((1, t), NEG, F32),) * 2, (jnp.ones((1, t), F32),) * 2)
    ms, pending = lax.fori_loop(0, i // 2, pair, start)
    last_done = jnp.maximum(2 * (i // 2) - 1, 0)

    @pl.when(i % 2 == 0)
    def _():
        _, a = soft(0, ms, True)
        accum(last_done, 1, pending)
        accum(i, 0, a)

    @pl.when(i % 2 == 1)
    def _():
        scores(i, 1)
        ms1, a0 = soft(0, ms, False)
        accum(last_done, 1, pending)
        _, a1 = soft(1, ms1, True)
        accum(i - 1, 0, a0)
        accum(i, 1, a1)

    o1 = acc_ref[0, :V_DIM, :] / acc_ref[0, V_DIM:V_DIM + 1, :]
    o2 = acc_ref[1, :V_DIM, :] / acc_ref[1, V_DIM:V_DIM + 1, :]
    od = o1 - _lambda_value(lam_ref, lam_init) * o2
    y = od * lax.rsqrt(jnp.mean(od * od, axis=0, keepdims=True) + EPS) * gsub_ref[...] * (1.0 - lam_init)
    o_ref[...] = y.T.astype(o_ref.dtype)


def _attn(q, kb, vt, slopes, lam, gsub_col, seq, lam_init):
    n = q.shape[0]
    b = n // seq
    t = _tile(seq, 512)
    nq = seq // t
    assert seq <= POS_SPLIT * 256, "key positions must split into two bf16-exact factors"
    pos = jnp.arange(seq)
    pf = jnp.zeros((seq, V_DIM), F32).at[:, 0].set(pos // POS_SPLIT).at[:, 1].set(pos % POS_SPLIT).astype(BF16)
    kern = functools.partial(_attn_kernel, t=t, lam_init=lam_init)
    return pl.pallas_call(
        kern,
        grid=(b, N_HEADS, nq),
        in_specs=[pl.BlockSpec(memory_space=pltpu.SMEM),
                  pl.BlockSpec(lam.shape, lambda bi, h, i: (0, 0)),
                  pl.BlockSpec((V_DIM, 1), lambda bi, h, i: (0, 0)),
                  pl.BlockSpec((t, V_DIM), lambda bi, h, i: (bi * nq + i, h)),
                  pl.BlockSpec((seq, V_DIM), lambda bi, h, i: (bi, h)),
                  pl.BlockSpec((V_DIM, seq), lambda bi, h, i: (h, bi)),
                  pl.BlockSpec((seq, V_DIM), lambda bi, h, i: (0, 0))],
        out_specs=pl.BlockSpec((t, V_DIM), lambda bi, h, i: (bi * nq + i, h)),
        out_shape=jax.ShapeDtypeStruct((n, ATT_WIDTH), BF16),
        scratch_shapes=[pltpu.VMEM((2, V_DIM + ONES_ROWS, t), F32),
                        pltpu.VMEM((2, t, t), F32),
                        pltpu.VMEM((2, t, t), F32),
                        pltpu.VMEM((2, t, t), BF16),
                        pltpu.VMEM((2, t, t), BF16)],
        compiler_params=_params("parallel", "parallel", "arbitrary"),
        name="prompt_attn",
    )(slopes, lam, gsub_col, q, kb, vt, pf)


def _page_copies(pt_ref, ck_hbm, cv_hbm, kbuf, vbuf, sem, step, slot, *, layer, n_chunks, pages_per_chunk, page):
    b = step // n_chunks
    c = step % n_chunks
    n_pages = n_chunks * pages_per_chunk
    copies = []
    for p in range(pages_per_chunk):
        phys = pt_ref[b * n_pages + c * pages_per_chunk + p]
        dst = pl.ds(p * page, page)
        copies.append(pltpu.make_async_copy(ck_hbm.at[layer, phys], kbuf.at[slot, dst], sem.at[0, slot]))
        copies.append(pltpu.make_async_copy(cv_hbm.at[layer, phys], vbuf.at[slot, dst], sem.at[1, slot]))
    return copies


def _sample_attn_kernel(pt_ref, slopes_ref, lam_ref, gsub_ref, q_ref, kn_ref, vn_ref, ck_hbm, cv_hbm, o_ref,
                        kbuf, vbuf, sem, m_ref, l_ref, acc_ref, *, layer, n_chunks, pages_per_chunk, page, t_new,
                        lam_init):
    b = pl.program_id(0)
    c = pl.program_id(1)
    step = b * n_chunks + c
    n_steps = pl.num_programs(0) * n_chunks
    slot = step % 2
    copies = functools.partial(_page_copies, pt_ref, ck_hbm, cv_hbm, kbuf, vbuf, sem, layer=layer,
                               n_chunks=n_chunks, pages_per_chunk=pages_per_chunk, page=page)

    def start_all(cps):
        for n, cp in enumerate(cps):
            cp.start(priority=n % 2)

    @pl.when(step == 0)
    def _():
        start_all(copies(step, slot))

    @pl.when(step + 1 < n_steps)
    def _():
        start_all(copies(step + 1, 1 - slot))

    @pl.when(c == 0)
    def _():
        m_ref[...] = jnp.full(m_ref.shape, NEG, F32)
        l_ref[...] = jnp.zeros(l_ref.shape, F32)
        acc_ref[...] = jnp.zeros(acc_ref.shape, F32)

    qh = q_ref[0]
    n_rows = qh.shape[0]
    ck = pages_per_chunk * page
    keys = ck // N_HEADS
    past = n_chunks * keys
    rows = lax.broadcasted_iota(I32, (n_rows, 1), 0)
    tok = rows % t_new
    head = rows // (2 * t_new)
    slope = jnp.zeros((n_rows, 1), F32)
    for hh in range(N_HEADS):
        slope = jnp.where(head == hh, slopes_ref[hh], slope)

    for cp in copies(step, slot):
        cp.wait()

    kb = kbuf[slot].astype(BF16)
    vb = vbuf[slot].astype(BF16)
    cols = lax.broadcasted_iota(I32, (n_rows, ck), 1)
    dist = (past + tok) - (c * keys + cols // N_HEADS)
    s = _dot_nt(qh, kb) - slope * dist.astype(F32)
    s = jnp.where(cols % N_HEADS == head, s, NEG)
    m, l, acc = _softmax_step(s, vb, m_ref[...], l_ref[...], acc_ref[...])
    m_ref[...] = m
    l_ref[...] = l
    acc_ref[...] = acc

    @pl.when(c == n_chunks - 1)
    def _():
        knb = kn_ref[0].astype(BF16)
        vnb = vn_ref[0].astype(BF16)
        ncol = lax.broadcasted_iota(I32, (n_rows, knb.shape[0]), 1)
        nkey = ncol // N_HEADS
        nd = tok - nkey
        sn = _dot_nt(qh, knb) - slope * nd.astype(F32)
        sn = jnp.where((nd >= 0) & (nkey < t_new) & (ncol % N_HEADS == head), sn, NEG)
        _, l2, acc2 = _softmax_step(sn, vnb, m, l, acc)
        o = acc2 / l2
        od = o - _lambda_value(lam_ref, lam_init) * pltpu.roll(o, n_rows - t_new, axis=0)
        o_ref[0] = _sub_norm(od, gsub_ref[...], lam_init).astype(o_ref.dtype)


def _sample_attn(q, kn, vn, cache_k, cache_v, layer, page_table, slopes, lam, gsub, lam_init):
    b, t_new, width = q.shape
    depth, n_phys, page = cache_k.shape[:3]
    n_pages = page_table.shape[1]
    ppc = 8 if n_pages % 8 == 0 else n_pages
    n_chunks = n_pages // ppc
    n_rows = N_HEADS * 2 * t_new
    pairs = page * N_HEADS
    qh = q.reshape(b, t_new, N_HEADS, V_DIM).transpose(0, 2, 1, 3)
    map_mask = (jnp.arange(V_DIM)[None, :] // HEAD_DIM == jnp.arange(2)[:, None]).astype(q.dtype)
    qh = (qh[:, :, None] * map_mask[None, None, :, None, :]).reshape(b, n_rows, V_DIM)
    new_rows = lambda a: jnp.pad(a, ((0, 0), (0, SUBLANES - t_new), (0, 0), (0, 0))).reshape(b, -1, V_DIM)
    kern = functools.partial(_sample_attn_kernel, layer=layer, n_chunks=n_chunks, pages_per_chunk=ppc, page=pairs,
                             t_new=t_new, lam_init=lam_init)
    grid_spec = pltpu.PrefetchScalarGridSpec(
        num_scalar_prefetch=1,
        grid=(b, n_chunks),
        in_specs=[pl.BlockSpec(memory_space=pltpu.SMEM),
                  pl.BlockSpec(lam.shape, lambda bi, c, pt: (0, 0)),
                  pl.BlockSpec((1, V_DIM), lambda bi, c, pt: (0, 0)),
                  pl.BlockSpec((1, n_rows, V_DIM), lambda bi, c, pt: (bi, 0, 0)),
                  pl.BlockSpec((1, SUBLANES * N_HEADS, V_DIM), lambda bi, c, pt: (bi, 0, 0)),
                  pl.BlockSpec((1, SUBLANES * N_HEADS, V_DIM), lambda bi, c, pt: (bi, 0, 0)),
                  pl.BlockSpec(memory_space=pl.ANY),
                  pl.BlockSpec(memory_space=pl.ANY)],
        out_specs=pl.BlockSpec((1, n_rows, V_DIM), lambda bi, c, pt: (bi, 0, 0)),
        scratch_shapes=[pltpu.VMEM((2, ppc * pairs, V_DIM), F32),
                        pltpu.VMEM((2, ppc * pairs, V_DIM), F32),
                        pltpu.SemaphoreType.DMA((2, 2)),
                        pltpu.VMEM((n_rows, 1), F32),
                        pltpu.VMEM((n_rows, 1), F32),
                        pltpu.VMEM((n_rows, V_DIM), F32)])
    out = pl.pallas_call(
        kern,
        grid_spec=grid_spec,
        out_shape=jax.ShapeDtypeStruct((b, n_rows, V_DIM), BF16),
        compiler_params=_params("arbitrary", "arbitrary"),
        name="sample_attn",
    )(page_table.reshape(-1), slopes, lam, gsub, qh, new_rows(kn), new_rows(vn),
      cache_k.reshape(depth, n_phys, pairs, V_DIM), cache_v.reshape(depth, n_phys, pairs, V_DIM))
    out = out.reshape(b, N_HEADS, 2, t_new, V_DIM)[:, :, 0]
    return out.transpose(0, 2, 1, 3).reshape(b, t_new, width)


def _outproj_kernel(x_ref, oatt_ref, up_ref, halo_ref, wbd_ref, pscale_ref, wout_ref, o_ref, ext_ref, *,
                    tm, n_prev, zero_first_halo):
    i = pl.program_id(1)
    halo = halo_ref[0]
    if zero_first_halo:
        halo = jnp.where(i == 0, 0.0, halo)
    cur = up_ref[0]
    ext_ref[0:HALO, :] = halo
    ext_ref[HALO:HALO + tm, :] = cur
    gw = cur.shape[1] // len(POOL_WINDOWS)
    pos1 = lax.broadcasted_iota(I32, (tm, gw), 0) + (i * tm + n_prev + 1)
    ds = []
    for g, w in enumerate(POOL_WINDOWS):
        cs = slice(g * gw, (g + 1) * gw)
        acc = cur[:, cs]
        for sft in range(1, w):
            acc = acc + ext_ref[HALO - sft:HALO - sft + tm, cs]
        cnt = jnp.minimum(pos1, w).astype(F32)
        ds.append(acc / cnt - cur[:, cs])
    d = jnp.concatenate(ds, axis=1)
    o_pool = _dot(d.astype(BF16), wbd_ref[...]) * pscale_ref[...]
    aw = oatt_ref.shape[2]
    o_ref[0] = (x_ref[0] + _dot(oatt_ref[0], wout_ref[0:aw, :])
                + _dot(o_pool.astype(BF16), wout_ref[aw:, :]))


def _outproj(x, o_att, up, halo_arr, wbd, pool_scale, w_out_b, *, n_prev, halo_from_up):
    b, t, d = x.shape
    aw = o_att.shape[2]
    pw = up.shape[2]
    tm = _tile(t, 512)
    if halo_from_up:
        halo_map = lambda bi, i: (bi, jnp.maximum(i * (tm // HALO) - 1, 0), 0)
    else:
        halo_map = lambda bi, i: (bi, 0, 0)
    kern = functools.partial(_outproj_kernel, tm=tm, n_prev=n_prev, zero_first_halo=halo_from_up)
    full = lambda shape: pl.BlockSpec(shape, lambda bi, i: (0,) * len(shape))
    return pl.pallas_call(
        kern,
        grid=(b, t // tm),
        in_specs=[pl.BlockSpec((1, tm, d), lambda bi, i: (bi, i, 0)),
                  pl.BlockSpec((1, tm, aw), lambda bi, i: (bi, i, 0)),
                  pl.BlockSpec((1, tm, pw), lambda bi, i: (bi, i, 0)),
                  pl.BlockSpec((1, HALO, pw), halo_map),
                  full((pw, pw)), full((1, pw)), full((aw + pw, d))],
        out_specs=pl.BlockSpec((1, tm, d), lambda bi, i: (bi, i, 0)),
        out_shape=jax.ShapeDtypeStruct((b, t, d), F32),
        scratch_shapes=[pltpu.VMEM((HALO + tm, pw), F32)],
        compiler_params=_params("parallel", "arbitrary"),
        name="pool_out_proj",
    )(x, o_att, up, halo_arr, wbd, pool_scale, w_out_b)


def _memkv_kernel(mem_ref, gmem_ref, w_ref, gxk_ref, gmat_ref, mk_ref, mv_ref):
    xw = mk_ref.shape[1]
    kv = _dot(_rms(mem_ref[...], gmem_ref[...]).astype(BF16), w_ref[...])
    kx = kv[:, :xw]
    mk_ref[...] = kx * lax.rsqrt(_group_meansq(kx, gmat_ref[...]) + EPS) * gxk_ref[...]
    mv_ref[...] = kv[:, xw:]


def _memkv(mem2d, g_mem, w_xkv_b, gxk, gmat):
    n, d = mem2d.shape
    xw = w_xkv_b.shape[1] // 2
    tm = _tile(n, 512)
    full = lambda shape: pl.BlockSpec(shape, lambda i: (0,) * len(shape))
    return pl.pallas_call(
        _memkv_kernel,
        grid=(n // tm,),
        in_specs=[pl.BlockSpec((tm, d), lambda i: (i, 0)), full((1, d)), full((d, 2 * xw)),
                  full((1, xw)), full((MXU_DIM, MXU_DIM))],
        out_specs=[pl.BlockSpec((tm, xw), lambda i: (i, 0))] * 2,
        out_shape=[jax.ShapeDtypeStruct((n, xw), F32)] * 2,
        compiler_params=_params("parallel"),
        name="mem_kv",
    )(mem2d, g_mem, w_xkv_b, gxk, gmat)


def _xattn_kernel(h_ref, mk_ref, mv_ref, gx_ref, wq_ref, gxq_ref, gmat_ref, wo_ref, o_ref):
    h1 = h_ref[0]
    qx = _dot(_rms(h1, gx_ref[...]).astype(BF16), wq_ref[...])
    qn = (qx * lax.rsqrt(_group_meansq(qx, gmat_ref[...]) + EPS) * gxq_ref[...]).astype(BF16)
    mk = mk_ref[0].astype(BF16)
    mv = mv_ref[0].astype(BF16)
    head_of_lane = lax.broadcasted_iota(I32, qn.shape, 1) // X_HEAD_DIM
    o = jnp.zeros(qn.shape, F32)
    for hh in range(X_HEADS):
        sel = head_of_lane == hh
        s = _dot_nt(jnp.where(sel, qn, jnp.zeros_like(qn)), mk)
        p = jnp.exp(s - jnp.max(s, axis=-1, keepdims=True))
        p = p / jnp.sum(p, axis=-1, keepdims=True)
        o = o + jnp.where(sel, _dot(p.astype(BF16), mv), 0.0)
    o_ref[0] = h1 + _dot(o.astype(BF16), wo_ref[...])


def _xattn(h1, mk, mv, g_xattn, w_xq_b, gxq, gmat, w_xo_b):
    b, t, d = h1.shape
    nm, xw = mk.shape[1], mk.shape[2]
    tq = _tile(t, 512)
    full = lambda shape: pl.BlockSpec(shape, lambda bi, i: (0,) * len(shape))
    return pl.pallas_call(
        _xattn_kernel,
        grid=(b, t // tq),
        in_specs=[pl.BlockSpec((1, tq, d), lambda bi, i: (bi, i, 0)),
                  pl.BlockSpec((1, nm, xw), lambda bi, i: (bi, 0, 0)),
                  pl.BlockSpec((1, nm, xw), lambda bi, i: (bi, 0, 0)),
                  full((1, d)), full((d, xw)), full((1, xw)), full((MXU_DIM, MXU_DIM)), full((xw, d))],
        out_specs=pl.BlockSpec((1, tq, d), lambda bi, i: (bi, i, 0)),
        out_shape=jax.ShapeDtypeStruct((b, t, d), F32),
        compiler_params=_params("parallel", "parallel"),
        name="mem_xattn",
    )(h1, mk, mv, g_xattn, w_xq_b, gxq, gmat, w_xo_b)


def _gather_rows(t, n_experts):
    return -(-(TOP_K * t + n_experts * (SUBLANES - 1)) // LANES) * LANES


def _slot_onehot(pos, n_rows):
    t = pos.shape[0]
    r = lax.broadcasted_iota(I32, (t, n_rows), 1)
    hit = r == pos[:, 0:1]
    for k in range(1, TOP_K):
        hit = hit | (r == pos[:, k:k + 1])
    return jnp.where(hit, 1.0, 0.0).astype(BF16)


def _route_kernel(h_ref, gffn_ref, wr_ref, br_ref, xg_ref, pos_ref, cnt_ref, *, n_experts):
    t = h_ref.shape[0]
    n_rows = xg_ref.shape[1]
    mb = _rms(h_ref[...], gffn_ref[...]).astype(BF16)
    logits = _dot(mb, wr_ref[...]) + br_ref[...]
    lane = lax.broadcasted_iota(I32, logits.shape, 1).astype(F32)
    work = logits
    sels, vals = [], []
    for _ in range(TOP_K):
        mx = jnp.max(work, axis=-1, keepdims=True)
        idx = jnp.min(jnp.where(work == mx, lane, float(LANES)), axis=-1, keepdims=True)
        sel = lane == idx
        sels.append(sel)
        vals.append(mx)
        work = jnp.where(sel, 2.0 * NEG, work)
    es = [jnp.exp(v - vals[0]) for v in vals]
    denom = es[0]
    for e in es[1:]:
        denom = denom + e
    hot = jnp.zeros(logits.shape, F32)
    gate_m = jnp.zeros(logits.shape, F32)
    for sel, e in zip(sels, es):
        hot = jnp.where(sel, 1.0, hot)
        gate_m = jnp.where(sel, e / denom, gate_m)
    ri = lax.broadcasted_iota(I32, (t, t), 0)
    ci = lax.broadcasted_iota(I32, (t, t), 1)
    rank = _dot(jnp.where(ci < ri, 1.0, 0.0).astype(BF16), hot.astype(BF16))
    count = jnp.sum(hot, axis=0, keepdims=True).astype(I32)
    c8 = (((count + (SUBLANES - 1)) >> 3) << 3).astype(F32)
    ei = lax.broadcasted_iota(I32, (LANES, LANES), 0)
    ej = lax.broadcasted_iota(I32, (LANES, LANES), 1)
    off = _dot(jnp.broadcast_to(c8, (SUBLANES, LANES)).astype(BF16), jnp.where(ei < ej, 1.0, 0.0).astype(BF16))[0:1]
    slot = off + rank
    lane_i = lax.broadcasted_iota(I32, logits.shape, 1)
    pos = jnp.full(logits.shape, -1, I32)
    for k, sel in enumerate(sels):
        pk = jnp.sum(jnp.where(sel, slot, 0.0), axis=-1, keepdims=True).astype(I32)
        pos = jnp.where(lane_i == k, pk, pos)
    pos_ref[...] = pos
    cnt_ref[0] = jnp.broadcast_to(count, (SUBLANES, LANES))
    g1 = gate_m.astype(BF16)
    r1 = gate_m - g1.astype(F32)
    g2 = r1.astype(BF16)
    g3 = (r1 - g2.astype(F32)).astype(BF16)
    gx = (g1.astype(F32) + pltpu.roll(g2.astype(F32), n_experts, axis=1)
          + pltpu.roll(g3.astype(F32), 2 * n_experts, axis=1)).astype(BF16)
    xg_ref[0] = _dot_tn(_slot_onehot(pos, n_rows), jnp.concatenate([mb, gx], axis=1))


def _route(h2d, g_ffn, wr_pad_b, br_pad, n_experts):
    n, d = h2d.shape
    t = _tile(n, 256)
    n_sub = n // t
    n_rows = _gather_rows(t, n_experts)
    kern = functools.partial(_route_kernel, n_experts=n_experts)
    full = lambda shape: pl.BlockSpec(shape, lambda i: (0,) * len(shape))
    return pl.pallas_call(
        kern,
        grid=(n_sub,),
        in_specs=[pl.BlockSpec((t, d), lambda i: (i, 0)), full((1, d)), full((d, LANES)), full((1, LANES))],
        out_specs=[pl.BlockSpec((1, n_rows, d + LANES), lambda i: (i, 0, 0)),
                   pl.BlockSpec((t, LANES), lambda i: (i, 0)),
                   pl.BlockSpec((1, SUBLANES, LANES), lambda i: (i, 0, 0))],
        out_shape=[jax.ShapeDtypeStruct((n_sub, n_rows, d + LANES), F32),
                   jax.ShapeDtypeStruct((n, LANES), I32),
                   jax.ShapeDtypeStruct((n_sub, SUBLANES, LANES), I32)],
        compiler_params=_params("parallel"),
        name="moe_route_gather",
    )(h2d, g_ffn, wr_pad_b, br_pad)


def _unit_plan(cnt, n_rows, n_experts):
    n_sub = cnt.shape[0]
    upt = n_rows // SUBLANES
    u = (cnt + (SUBLANES - 1)) // SUBLANES
    uoff = jnp.cumsum(u, axis=1) - u
    used = jnp.sum(u, axis=1, keepdims=True)
    u_all = jnp.concatenate([u, upt - used], axis=1)
    uoff_all = jnp.concatenate([uoff, used], axis=1)
    flat_cnt = u_all.T.reshape(-1)
    cum = jnp.cumsum(flat_cnt)
    start = cum - flat_cnt
    k = jnp.arange(upt, dtype=I32)
    grp = jnp.sum(uoff_all[:, None, 1:] <= k[None, :, None], axis=-1)
    shift = start.reshape(n_experts + 1, n_sub).T - uoff_all
    hit = grp[:, :, None] == jnp.arange(n_experts + 1, dtype=I32)[None, None, :]
    dest = k[None, :] + jnp.sum(jnp.where(hit, shift[:, None, :], 0), axis=-1)
    n_units = n_sub * upt
    unit_src = jnp.zeros((n_units,), I32).at[dest.reshape(-1)].set(jnp.arange(n_units, dtype=I32))
    ustart = jnp.concatenate([start[::n_sub], cum[-1:]]).astype(I32)
    return ustart, unit_src


def _expert_stream(k, ustart_ref, usrc_ref, xg_hbm, yg_hbm, xbuf, ybuf, in_sem, out_sem, bgu_ref, bd_ref, wgu_b, wd_b,
                   *, n_experts, upb, d_model):
    e = pl.program_id(0)
    u0 = ustart_ref[e]
    u1 = ustart_ref[e + 1]
    dff = wd_b.shape[0]

    def in_copy(j, slot, i):
        return pltpu.make_async_copy(xg_hbm.at[usrc_ref[j]], xbuf.at[slot, pl.ds(i * SUBLANES, SUBLANES)],
                                     in_sem.at[k, slot])

    def out_copy(j, i):
        return pltpu.make_async_copy(ybuf.at[pl.ds(i * SUBLANES, SUBLANES)], yg_hbm.at[usrc_ref[j]], out_sem.at[k])

    def n_valid(blk):
        return jnp.minimum(upb, u1 - (u0 + blk * upb))

    def for_units(blk, fn):
        base = u0 + blk * upb
        nv = n_valid(blk)

        @pl.when(nv == upb)
        def _():
            for i in range(upb):
                fn(base + i, i)

        @pl.when(nv < upb)
        def _():
            def body(i, c):
                fn(base + i, i)
                return c
            lax.fori_loop(0, nv, body, 0)

    @pl.when(e == 0)
    def _():
        xbuf[...] = jnp.zeros(xbuf.shape, F32)

    @pl.when(e < n_experts)
    def _():
        n_blk = (u1 - u0 + upb - 1) // upb

        @pl.when(n_blk > 0)
        def _():
            for_units(0, lambda j, i: in_copy(j, 0, i).start())

        def block(blk, c):
            slot = blk % 2

            @pl.when(blk + 1 < n_blk)
            def _():
                for_units(blk + 1, lambda j, i: in_copy(j, 1 - slot, i).start())

            for_units(blk, lambda j, i: in_copy(j, slot, i).wait())
            x = xbuf[slot]
            gcols = x[:, d_model:]
            lane = lax.broadcasted_iota(I32, gcols.shape, 1)
            own = (lane % n_experts == e) & (lane < 3 * n_experts)
            gate_w = jnp.sum(jnp.where(own, gcols, 0.0), axis=-1, keepdims=True)
            xb = x[:, :d_model].astype(BF16)
            hidden = []
            for c0 in range(0, dff, MXU_DIM):
                gs = slice(c0, c0 + MXU_DIM)
                us = slice(dff + c0, dff + c0 + MXU_DIM)
                gate = jnp.minimum(_dot(xb, wgu_b[:, gs]) + bgu_ref[0, :, gs], SWIGLU_LIMIT)
                up = jnp.clip(_dot(xb, wgu_b[:, us]) + bgu_ref[0, :, us], -SWIGLU_LIMIT, SWIGLU_LIMIT)
                glu = gate * (1.0 / (1.0 + jnp.exp(-SWIGLU_ALPHA * gate)))
                hidden.append(((up + 1.0) * glu).astype(BF16))
            y = (_dot(jnp.concatenate(hidden, axis=1), wd_b[...]) + bd_ref[0]) * gate_w

            @pl.when(blk > 0)
            def _():
                for_units(blk - 1, lambda j, i: out_copy(j, i).wait())

            ybuf[...] = y
            for_units(blk, lambda j, i: out_copy(j, i).start())
            return c

        lax.fori_loop(0, n_blk, block, 0)

        @pl.when(n_blk > 0)
        def _():
            for_units(n_blk - 1, lambda j, i: out_copy(j, i).wait())

    @pl.when(e == n_experts)
    def _():
        ybuf[...] = jnp.zeros(ybuf.shape, F32)

        def start(j, c):
            out_copy(j, 0).start()
            return c

        def wait(j, c):
            out_copy(j, 0).wait()
            return c
        lax.fori_loop(u0, u1, start, 0)
        lax.fori_loop(u0, u1, wait, 0)


def _expert_kernel(*refs, n_experts, upbs, d_model):
    n = len(upbs)
    plans = refs[:2 * n]
    xg_hbms = refs[2 * n:3 * n]
    wgu_ref, bgu_ref, wd_ref, bd_ref = refs[3 * n:3 * n + 4]
    yg_hbms = refs[3 * n + 4:4 * n + 4]
    xbufs = refs[4 * n + 4:5 * n + 4]
    ybufs = refs[5 * n + 4:6 * n + 4]
    wgu_b, wd_b, in_sem, out_sem = refs[6 * n + 4:]

    @pl.when(pl.program_id(0) < n_experts)
    def _():
        rc = 128
        for r in range(0, wgu_b.shape[0], rc):
            wgu_b[r:r + rc, :] = wgu_ref[0, r:r + rc, :].astype(BF16)
        for r in range(0, wd_b.shape[0], rc):
            wd_b[r:r + rc, :] = wd_ref[0, r:r + rc, :].astype(BF16)

    for k in range(n):
        _expert_stream(k, plans[2 * k], plans[2 * k + 1], xg_hbms[k], yg_hbms[k], xbufs[k], ybufs[k], in_sem, out_sem,
                       bgu_ref, bd_ref, wgu_b, wd_b, n_experts=n_experts, upb=upbs[k], d_model=d_model)


def _experts(plans, xgs, trs, w_gu, b_gu, w_d, b_d):
    n = len(xgs)
    n_experts, d_model, dff2 = w_gu.shape
    dff = w_d.shape[1]
    dx = xgs[0].shape[2]
    units = [x.shape[0] * x.shape[1] // SUBLANES for x in xgs]
    kern = functools.partial(_expert_kernel, n_experts=n_experts, upbs=tuple(tr // SUBLANES for tr in trs),
                             d_model=d_model)
    last = n_experts - 1
    wmap = lambda e, *_: (jnp.minimum(e, last), 0, 0)
    grid_spec = pltpu.PrefetchScalarGridSpec(
        num_scalar_prefetch=2 * n,
        grid=(n_experts + 1,),
        in_specs=[pl.BlockSpec(memory_space=pl.ANY)] * n + [
            pl.BlockSpec((1, d_model, dff2), wmap), pl.BlockSpec((1, 1, dff2), wmap),
            pl.BlockSpec((1, dff, d_model), wmap), pl.BlockSpec((1, 1, d_model), wmap)],
        out_specs=[pl.BlockSpec(memory_space=pl.ANY)] * n,
        scratch_shapes=[pltpu.VMEM((2, tr, dx), F32) for tr in trs] + [pltpu.VMEM((tr, d_model), F32) for tr in trs] + [
            pltpu.VMEM((d_model, dff2), BF16),
            pltpu.VMEM((dff, d_model), BF16),
            pltpu.SemaphoreType.DMA((n, 2)),
            pltpu.SemaphoreType.DMA((n,))])
    ygs = pl.pallas_call(
        kern,
        grid_spec=grid_spec,
        out_shape=[jax.ShapeDtypeStruct((u, SUBLANES, d_model), F32) for u in units],
        compiler_params=_params("arbitrary"),
        name="moe_experts",
    )(*[a for plan in plans for a in plan], *[x.reshape(u, SUBLANES, dx) for x, u in zip(xgs, units)],
      w_gu, b_gu.reshape(n_experts, 1, dff2), w_d, b_d.reshape(n_experts, 1, d_model))
    return [y.reshape(x.shape[0], x.shape[1], d_model) for y, x in zip(ygs, xgs)]


def _combine_kernel(yg_ref, pos_ref, h_ref, o_ref):
    onehot = _slot_onehot(pos_ref[...], yg_ref.shape[1])
    o_ref[...] = h_ref[...] + _dot(onehot, yg_ref[0].astype(BF16))


def _combine(yg, pos, h2d):
    n_sub, n_rows, d = yg.shape
    n = h2d.shape[0]
    t = n // n_sub
    return pl.pallas_call(
        _combine_kernel,
        grid=(n_sub,),
        in_specs=[pl.BlockSpec((1, n_rows, d), lambda i: (i, 0, 0)),
                  pl.BlockSpec((t, LANES), lambda i: (i, 0)),
                  pl.BlockSpec((t, d), lambda i: (i, 0))],
        out_specs=pl.BlockSpec((t, d), lambda i: (i, 0)),
        out_shape=jax.ShapeDtypeStruct((n, d), F32),
        compiler_params=_params("parallel"),
        name="moe_combine",
    )(yg, pos, h2d)


def _moe(hs, g_ffn, wr_pad_b, br_pad, w_gu, b_gu, w_d, b_d):
    n_experts = w_gu.shape[0]
    h2ds = [h.reshape(-1, h.shape[-1]) for h in hs]
    routed = [_route(h2d, g_ffn, wr_pad_b, br_pad, n_experts) for h2d in h2ds]
    plans = [_unit_plan(cnt[:, 0, :n_experts], xg.shape[1], n_experts) for xg, _, cnt in routed]
    trs = [512 if h2d.shape[0] >= 4096 else 128 for h2d in h2ds]
    ygs = _experts(plans, [r[0] for r in routed], trs, w_gu, b_gu, w_d, b_d)
    return [_combine(yg, r[1], h2d).reshape(h.shape) for yg, r, h2d, h in zip(ygs, routed, h2ds, hs)]


def kernel(x_prompt, x_sample, mem_prompt, cache_k, cache_v, cache_mem_k, cache_mem_v, state_pool, page_table,
           g_mix, w_in, g_q, g_k, lam, g_sub, w_pool, pool_scale, w_out,
           g_xattn, g_mem, w_xq, w_xkv, g_xq, g_xk, w_xo,
           g_ffn, w_router, b_router, w_gate_up, b_gate_up, w_down, b_down):
    depth = w_in.shape[0]
    b_p, s_p, d = x_prompt.shape
    b_s, t_s, _ = x_sample.shape
    n_mem = mem_prompt.shape[1]
    n_experts = w_router.shape[2]
    assert LANES % n_experts == 0 and 3 * n_experts <= LANES
    assert ATT_WIDTH % MXU_DIM == 0 and (X_HEADS * X_HEAD_DIM) % MXU_DIM == 0

    slopes = jnp.asarray([2.0 ** (-8.0 * (h + 1) / N_HEADS) for h in range(N_HEADS)], F32)
    grp = jnp.arange(MXU_DIM) // HEAD_DIM
    gmat = (grp[:, None] == grp[None, :]).astype(BF16)
    row = lambda a: a.reshape(1, -1).astype(F32)

    hp, hs = x_prompt, x_sample
    outs = [[] for _ in range(8)]
    for l in range(depth):
        lam_init = 0.8 - 0.6 * math.exp(-0.3 * l)
        w_in_b = w_in[l].astype(BF16)
        w_out_b = w_out[l].astype(BF16)
        w_xq_b = w_xq[l].astype(BF16)
        w_xkv_b = w_xkv[l].astype(BF16)
        w_xo_b = w_xo[l].astype(BF16)
        gqk = jnp.concatenate([jnp.tile(g_q[l].reshape(-1), N_HEADS), jnp.tile(g_k[l].reshape(-1), N_HEADS)])[None, :]
        gxq = (jnp.tile(g_xq[l], X_HEADS) * (X_HEAD_DIM ** -0.5))[None, :]
        gxk = jnp.tile(g_xk[l], X_HEADS)[None, :]
        wbd = jax.scipy.linalg.block_diag(*[w_pool[l, g] for g in range(len(POOL_WINDOWS))]).astype(BF16)
        wr_pad_b = jnp.pad(w_router[l], ((0, 0), (0, LANES - n_experts))).astype(BF16)
        br_pad = jnp.pad(b_router[l].astype(F32), (0, LANES - n_experts), constant_values=NEG)[None, :]
        moe_w = (row(g_ffn[l]), wr_pad_b, br_pad, w_gate_up[l], b_gate_up[l], w_down[l], b_down[l])
        gsub = row(g_sub[l])

        q_b, k_b, v_t, k32, v32, up = _proj(hp.reshape(b_p * s_p, d), row(g_mix[l]), w_in_b, gqk, gmat)
        o_att = _attn(q_b, k_b, v_t, slopes, lam[l], g_sub[l].reshape(-1, 1).astype(F32), s_p, lam_init)
        up3 = up.reshape(b_p, s_p, -1)
        h1 = _outproj(hp, o_att.reshape(b_p, s_p, -1), up3, up3, wbd, row(pool_scale[l]), w_out_b, n_prev=0,
                      halo_from_up=True)
        mk, mv = _memkv(mem_prompt.reshape(b_p * n_mem, d), row(g_mem[l]), w_xkv_b, gxk, gmat)
        h2 = _xattn(h1, mk.reshape(b_p, n_mem, -1), mv.reshape(b_p, n_mem, -1), row(g_xattn[l]), w_xq_b, gxq, gmat,
                    w_xo_b)
        outs[0].append(k32.reshape(b_p, s_p, N_HEADS, V_DIM))
        outs[1].append(v32.reshape(b_p, s_p, N_HEADS, V_DIM))
        outs[2].append(up3[:, s_p - POOL_BUF:])
        outs[3].append(mk.reshape(b_p, n_mem, X_HEADS, X_HEAD_DIM))
        outs[4].append(mv.reshape(b_p, n_mem, X_HEADS, X_HEAD_DIM))

        q_s, _, _, k32_s, v32_s, up_s = _proj(hs.reshape(b_s * t_s, d), row(g_mix[l]), w_in_b, gqk, gmat)
        kn = k32_s.reshape(b_s, t_s, N_HEADS, V_DIM)
        vn = v32_s.reshape(b_s, t_s, N_HEADS, V_DIM)
        o_att_s = _sample_attn(q_s.reshape(b_s, t_s, -1), kn, vn, cache_k, cache_v, l, page_table, slopes, lam[l],
                               gsub, lam_init)
        ups = up_s.reshape(b_s, t_s, -1)
        halo_s = jnp.pad(state_pool[l], ((0, 0), (HALO - POOL_BUF, 0), (0, 0)))
        h1s = _outproj(hs, o_att_s, ups, halo_s, wbd, row(pool_scale[l]), w_out_b, n_prev=POOL_BUF, halo_from_up=False)
        h2s = _xattn(h1s, cache_mem_k[l].reshape(b_s, n_mem, -1), cache_mem_v[l].reshape(b_s, n_mem, -1),
                     row(g_xattn[l]), w_xq_b, gxq, gmat, w_xo_b)
        hp, hs = _moe([h2, h2s], *moe_w)
        outs[5].append(kn.reshape(b_s, t_s, N_HEADS, V_DIM))
        outs[6].append(vn.reshape(b_s, t_s, N_HEADS, V_DIM))
        outs[7].append(jnp.concatenate([state_pool[l], ups], axis=1)[:, t_s:])

    stacked = [jnp.stack(o) for o in outs]
    return (hp, hs, *stacked)
```

```python
import functools
import math

import jax
import jax.numpy as jnp
from jax import lax
from jax.experimental import pallas as pl
from jax.experimental.pallas import tpu as pltpu

F32 = jnp.float32
BF16 = jnp.bfloat16
I32 = jnp.int32

EPS = 1e-6
N_HEADS = 4
HEAD_DIM = 64
V_DIM = 2 * HEAD_DIM
ATT_WIDTH = N_HEADS * V_DIM
POOL_WINDOWS = (2, 4, 8, 16)
POOL_BUF = max(POOL_WINDOWS) - 1
HALO = POOL_BUF + 1
X_HEADS = 4
X_HEAD_DIM = 64
TOP_K = 4
SWIGLU_LIMIT = 7.0
SWIGLU_ALPHA = 1.702
NEG = -1e30

LANES = 128
SUBLANES = 8
MXU_DIM = 256
VMEM_LIMIT = 56 * 1024 * 1024


def _params(*sem):
    return pltpu.CompilerParams(dimension_semantics=sem, vmem_limit_bytes=VMEM_LIMIT)


def _tile(n, pref):
    return pref if n % pref == 0 else n


def _rms(x, g):
    return x * lax.rsqrt(jnp.mean(x * x, axis=-1, keepdims=True) + EPS) * g


def _dot(a, b):
    return jnp.dot(a, b, preferred_element_type=F32)


def _dot_nt(a, b):
    return lax.dot_general(a, b, (((1,), (1,)), ((), ())), preferred_element_type=F32)


def _dot_tn(a, b):
    return lax.dot_general(a, b, (((0,), (0,)), ((), ())), preferred_element_type=F32)


def _group_meansq(u, gmat):
    sq = u * u
    hi = sq.astype(BF16)
    lo = (sq - hi.astype(F32)).astype(BF16)
    outs = []
    for c in range(u.shape[1] // MXU_DIM):
        sl = slice(c * MXU_DIM, (c + 1) * MXU_DIM)
        outs.append(_dot(hi[:, sl], gmat) + _dot(lo[:, sl], gmat))
    return jnp.concatenate(outs, axis=1) * (1.0 / HEAD_DIM)


def _proj_kernel(x_ref, gmix_ref, w_ref, gqk_ref, gmat_ref, q_ref, kb_ref, vt_ref, k_ref, v_ref, up_ref):
    aw = ATT_WIDTH
    xn = _rms(x_ref[...], gmix_ref[...])
    u = _dot(xn.astype(BF16), w_ref[...])
    qk = u[:, :2 * aw]
    qkn = qk * lax.rsqrt(_group_meansq(qk, gmat_ref[...]) + EPS) * gqk_ref[...]
    kn = qkn[:, aw:]
    v = u[:, 2 * aw:3 * aw]
    for h in range(N_HEADS):
        hs = slice(h * V_DIM, (h + 1) * V_DIM)
        k_ref[:, h, :] = kn[:, hs]
        v_ref[:, h, :] = v[:, hs]
    up_ref[...] = u[:, 3 * aw:]
    q_ref[...] = (qkn[:, :aw] * (HEAD_DIM ** -0.5)).astype(BF16)
    kb_ref[...] = kn.astype(BF16)
    vt_ref[...] = v.T.astype(BF16)


def _proj(x2d, g_mix, w_in_b, gqk, gmat):
    n, d = x2d.shape
    wtot = w_in_b.shape[1]
    aw = ATT_WIDTH
    tm = _tile(n, 512)
    full = lambda shape: pl.BlockSpec(shape, lambda i: (0,) * len(shape))
    rows = lambda w: pl.BlockSpec((tm, w), lambda i: (i, 0))
    heads = pl.BlockSpec((tm, N_HEADS, V_DIM), lambda i: (i, 0, 0))
    return pl.pallas_call(
        _proj_kernel,
        grid=(n // tm,),
        in_specs=[rows(d), full((1, d)), full((d, wtot)), full((1, 2 * aw)), full((MXU_DIM, MXU_DIM))],
        out_specs=[rows(aw), rows(aw), pl.BlockSpec((aw, tm), lambda i: (0, i)), heads, heads, rows(wtot - 3 * aw)],
        out_shape=[jax.ShapeDtypeStruct((n, aw), BF16),
                   jax.ShapeDtypeStruct((n, aw), BF16),
                   jax.ShapeDtypeStruct((aw, n), BF16),
                   jax.ShapeDtypeStruct((n, N_HEADS, V_DIM), F32),
                   jax.ShapeDtypeStruct((n, N_HEADS, V_DIM), F32),
                   jax.ShapeDtypeStruct((n, wtot - 3 * aw), F32)],
        compiler_params=_params("parallel"),
        name="in_proj",
    )(x2d, g_mix, w_in_b, gqk, gmat)


def _lambda_value(lam_ref, lam_init):
    lf = lam_ref[...]
    a = jnp.sum(lf[0:1] * lf[1:2], axis=-1, keepdims=True)
    b = jnp.sum(lf[2:3] * lf[3:4], axis=-1, keepdims=True)
    return jnp.exp(a) - jnp.exp(b) + lam_init


def _sub_norm(od, gsub, lam_init):
    return _rms(od, gsub) * (1.0 - lam_init)


def _softmax_step(s, vb, m, l, acc):
    m_new = jnp.maximum(m, jnp.max(s, axis=-1, keepdims=True))
    alpha = jnp.exp(m - m_new)
    p = jnp.exp(s - m_new)
    l = alpha * l + jnp.sum(p, axis=-1, keepdims=True)
    acc = alpha * acc + _dot(p.astype(BF16), vb)
    return m_new, l, acc


POS_SPLIT = 64
ONES_ROWS = 2 * SUBLANES


def _attn_kernel(slopes_ref, lam_ref, gsub_ref, q_ref, k_ref, vt_ref, pf_ref, o_ref, acc_ref, s0_ref, s1_ref,
                 *, t, lam_init):
    h = pl.program_id(1)
    i = pl.program_id(2)
    slope = slopes_ref[h]
    q = q_ref[...]
    lane = lax.broadcasted_iota(I32, q.shape, 1)
    zero = jnp.zeros_like(q)
    feat = jnp.where(lane == 0, slope * POS_SPLIT, jnp.where(lane == 1, slope, 0.0)).astype(BF16)
    qa = (jnp.concatenate([jnp.where(lane < HEAD_DIM, q, zero), feat], axis=1),
          jnp.concatenate([jnp.where(lane >= HEAD_DIM, q, zero), feat], axis=1))
    acc_ref[...] = jnp.zeros(acc_ref.shape, F32)
    s_bufs = (s0_ref, s1_ref)

    def scores(j, buf):
        k0 = pl.multiple_of(j * t, t)
        kaug = jnp.concatenate([k_ref[pl.ds(k0, t), :], pf_ref[pl.ds(k0, t), :]], axis=1)
        for c in range(2):
            s_bufs[buf][c] = _dot_nt(kaug, qa[c])

    def absorb(j, buf, ms, masked):
        k0 = pl.multiple_of(j * t, t)
        vt = jnp.concatenate([vt_ref[:, pl.ds(k0, t)], jnp.ones((ONES_ROWS, t), BF16)], axis=0)
        out = []
        for c in range(2):
            s = s_bufs[buf][c]
            if masked:
                kr = lax.broadcasted_iota(I32, s.shape, 0)
                qc = lax.broadcasted_iota(I32, s.shape, 1)
                s = jnp.where(kr <= qc, s, NEG)
            m_new = jnp.maximum(ms[c], jnp.max(s, axis=0, keepdims=True))
            alpha = jnp.exp(ms[c] - m_new)
            acc_ref[c] = alpha * acc_ref[c] + _dot(vt, jnp.exp(s - m_new).astype(BF16))
            out.append(m_new)
        return tuple(out)

    def pair(jj, ms):
        j = 2 * jj
        scores(j + 1, 1)
        ms = absorb(j, 0, ms, False)
        scores(j + 2, 0)
        return absorb(j + 1, 1, ms, False)

    scores(0, 0)
    ms = lax.fori_loop(0, i // 2, pair, (jnp.full((1, t), NEG, F32),) * 2)

    @pl.when(i % 2 == 0)
    def _():
        absorb(i, 0, ms, True)

    @pl.when(i % 2 == 1)
    def _():
        scores(i, 1)
        absorb(i, 1, absorb(i - 1, 0, ms, False), True)

    o1 = acc_ref[0, :V_DIM, :] / acc_ref[0, V_DIM:V_DIM + 1, :]
    o2 = acc_ref[1, :V_DIM, :] / acc_ref[1, V_DIM:V_DIM + 1, :]
    od = o1 - _lambda_value(lam_ref, lam_init) * o2
    y = od * lax.rsqrt(jnp.mean(od * od, axis=0, keepdims=True) + EPS) * gsub_ref[...] * (1.0 - lam_init)
    o_ref[...] = y.T.astype(o_ref.dtype)


def _attn(q, kb, vt, slopes, lam, gsub_col, seq, lam_init):
    n = q.shape[0]
    b = n // seq
    t = _tile(seq, 512)
    nq = seq // t
    assert seq <= POS_SPLIT * 256, "key positions must split into two bf16-exact factors"
    pos = jnp.arange(seq)
    pf = jnp.zeros((seq, V_DIM), F32).at[:, 0].set(pos // POS_SPLIT).at[:, 1].set(pos % POS_SPLIT).astype(BF16)
    kern = functools.partial(_attn_kernel, t=t, lam_init=lam_init)
    return pl.pallas_call(
        kern,
        grid=(b, N_HEADS, nq),
        in_specs=[pl.BlockSpec(memory_space=pltpu.SMEM),
                  pl.BlockSpec(lam.shape, lambda bi, h, i: (0, 0)),
                  pl.BlockSpec((V_DIM, 1), lambda bi, h, i: (0, 0)),
                  pl.BlockSpec((t, V_DIM), lambda bi, h, i: (bi * nq + i, h)),
                  pl.BlockSpec((seq, V_DIM), lambda bi, h, i: (bi, h)),
                  pl.BlockSpec((V_DIM, seq), lambda bi, h, i: (h, bi)),
                  pl.BlockSpec((seq, V_DIM), lambda bi, h, i: (0, 0))],
        out_specs=pl.BlockSpec((t, V_DIM), lambda bi, h, i: (bi * nq + i, h)),
        out_shape=jax.ShapeDtypeStruct((n, ATT_WIDTH), BF16),
        scratch_shapes=[pltpu.VMEM((2, V_DIM + ONES_ROWS, t), F32),
                        pltpu.VMEM((2, t, t), F32),
                        pltpu.VMEM((2, t, t), F32)],
        compiler_params=_params("parallel", "parallel", "arbitrary"),
        name="prompt_attn",
    )(slopes, lam, gsub_col, q, kb, vt, pf)


def _page_copies(pt_ref, ck_hbm, cv_hbm, kbuf, vbuf, sem, step, slot, *, layer, n_chunks, pages_per_chunk, page):
    b = step // n_chunks
    c = step % n_chunks
    n_pages = n_chunks * pages_per_chunk
    copies = []
    for p in range(pages_per_chunk):
        phys = pt_ref[b * n_pages + c * pages_per_chunk + p]
        dst = pl.ds(p * page, page)
        copies.append(pltpu.make_async_copy(ck_hbm.at[layer, phys], kbuf.at[slot, dst], sem.at[0, slot]))
        copies.append(pltpu.make_async_copy(cv_hbm.at[layer, phys], vbuf.at[slot, dst], sem.at[1, slot]))
    return copies


def _sample_attn_kernel(pt_ref, slopes_ref, lam_ref, gsub_ref, q_ref, kn_ref, vn_ref, ck_hbm, cv_hbm, o_ref,
                        kbuf, vbuf, sem, m_ref, l_ref, acc_ref, *, layer, n_chunks, pages_per_chunk, page, t_new,
                        lam_init):
    b = pl.program_id(0)
    c = pl.program_id(1)
    step = b * n_chunks + c
    n_steps = pl.num_programs(0) * n_chunks
    slot = step % 2
    copies = functools.partial(_page_copies, pt_ref, ck_hbm, cv_hbm, kbuf, vbuf, sem, layer=layer,
                               n_chunks=n_chunks, pages_per_chunk=pages_per_chunk, page=page)

    def start_all(cps):
        for n, cp in enumerate(cps):
            cp.start(priority=n % 2)

    @pl.when(step == 0)
    def _():
        start_all(copies(step, slot))

    @pl.when(step + 1 < n_steps)
    def _():
        start_all(copies(step + 1, 1 - slot))

    @pl.when(c == 0)
    def _():
        m_ref[...] = jnp.full(m_ref.shape, NEG, F32)
        l_ref[...] = jnp.zeros(l_ref.shape, F32)
        acc_ref[...] = jnp.zeros(acc_ref.shape, F32)

    qh = q_ref[0]
    n_rows = qh.shape[0]
    ck = pages_per_chunk * page
    keys = ck // N_HEADS
    past = n_chunks * keys
    rows = lax.broadcasted_iota(I32, (n_rows, 1), 0)
    tok = rows % t_new
    head = rows // (2 * t_new)
    slope = jnp.zeros((n_rows, 1), F32)
    for hh in range(N_HEADS):
        slope = jnp.where(head == hh, slopes_ref[hh], slope)

    for cp in copies(step, slot):
        cp.wait()

    kb = kbuf[slot].astype(BF16)
    vb = vbuf[slot].astype(BF16)
    cols = lax.broadcasted_iota(I32, (n_rows, ck), 1)
    dist = (past + tok) - (c * keys + cols // N_HEADS)
    s = _dot_nt(qh, kb) - slope * dist.astype(F32)
    s = jnp.where(cols % N_HEADS == head, s, NEG)
    m, l, acc = _softmax_step(s, vb, m_ref[...], l_ref[...], acc_ref[...])
    m_ref[...] = m
    l_ref[...] = l
    acc_ref[...] = acc

    @pl.when(c == n_chunks - 1)
    def _():
        knb = kn_ref[0].astype(BF16)
        vnb = vn_ref[0].astype(BF16)
        ncol = lax.broadcasted_iota(I32, (n_rows, knb.shape[0]), 1)
        nkey = ncol // N_HEADS
        nd = tok - nkey
        sn = _dot_nt(qh, knb) - slope * nd.astype(F32)
        sn = jnp.where((nd >= 0) & (nkey < t_new) & (ncol % N_HEADS == head), sn, NEG)
        _, l2, acc2 = _softmax_step(sn, vnb, m, l, acc)
        o = acc2 / l2
        od = o - _lambda_value(lam_ref, lam_init) * pltpu.roll(o, n_rows - t_new, axis=0)
        o_ref[0] = _sub_norm(od, gsub_ref[...], lam_init).astype(o_ref.dtype)


def _sample_attn(q, kn, vn, cache_k, cache_v, layer, page_table, slopes, lam, gsub, lam_init):
    b, t_new, width = q.shape
    depth, n_phys, page = cache_k.shape[:3]
    n_pages = page_table.shape[1]
    ppc = 16 if n_pages % 16 == 0 else n_pages
    n_chunks = n_pages // ppc
    n_rows = N_HEADS * 2 * t_new
    pairs = page * N_HEADS
    qh = q.reshape(b, t_new, N_HEADS, V_DIM).transpose(0, 2, 1, 3)
    map_mask = (jnp.arange(V_DIM)[None, :] // HEAD_DIM == jnp.arange(2)[:, None]).astype(q.dtype)
    qh = (qh[:, :, None] * map_mask[None, None, :, None, :]).reshape(b, n_rows, V_DIM)
    new_rows = lambda a: jnp.pad(a, ((0, 0), (0, SUBLANES - t_new), (0, 0), (0, 0))).reshape(b, -1, V_DIM)
    kern = functools.partial(_sample_attn_kernel, layer=layer, n_chunks=n_chunks, pages_per_chunk=ppc, page=pairs,
                             t_new=t_new, lam_init=lam_init)
    grid_spec = pltpu.PrefetchScalarGridSpec(
        num_scalar_prefetch=1,
        grid=(b, n_chunks),
        in_specs=[pl.BlockSpec(memory_space=pltpu.SMEM),
                  pl.BlockSpec(lam.shape, lambda bi, c, pt: (0, 0)),
                  pl.BlockSpec((1, V_DIM), lambda bi, c, pt: (0, 0)),
                  pl.BlockSpec((1, n_rows, V_DIM), lambda bi, c, pt: (bi, 0, 0)),
                  pl.BlockSpec((1, SUBLANES * N_HEADS, V_DIM), lambda bi, c, pt: (bi, 0, 0)),
                  pl.BlockSpec((1, SUBLANES * N_HEADS, V_DIM), lambda bi, c, pt: (bi, 0, 0)),
                  pl.BlockSpec(memory_space=pl.ANY),
                  pl.BlockSpec(memory_space=pl.ANY)],
        out_specs=pl.BlockSpec((1, n_rows, V_DIM), lambda bi, c, pt: (bi, 0, 0)),
        scratch_shapes=[pltpu.VMEM((2, ppc * pairs, V_DIM), F32),
                        pltpu.VMEM((2, ppc * pairs, V_DIM), F32),
                        pltpu.SemaphoreType.DMA((2, 2)),
                        pltpu.VMEM((n_rows, 1), F32),
                        pltpu.VMEM((n_rows, 1), F32),
                        pltpu.VMEM((n_rows, V_DIM), F32)])
    out = pl.pallas_call(
        kern,
        grid_spec=grid_spec,
        out_shape=jax.ShapeDtypeStruct((b, n_rows, V_DIM), BF16),
        compiler_params=_params("arbitrary", "arbitrary"),
        name="sample_attn",
    )(page_table.reshape(-1), slopes, lam, gsub, qh, new_rows(kn), new_rows(vn),
      cache_k.reshape(depth, n_phys, pairs, V_DIM), cache_v.reshape(depth, n_phys, pairs, V_DIM))
    out = out.reshape(b, N_HEADS, 2, t_new, V_DIM)[:, :, 0]
    return out.transpose(0, 2, 1, 3).reshape(b, t_new, width)


def _outproj_kernel(x_ref, oatt_ref, up_ref, halo_ref, wbd_ref, pscale_ref, wout_ref, o_ref, ext_ref, *,
                    tm, n_prev, zero_first_halo):
    i = pl.program_id(1)
    halo = halo_ref[0]
    if zero_first_halo:
        halo = jnp.where(i == 0, 0.0, halo)
    cur = up_ref[0]
    ext_ref[0:HALO, :] = halo
    ext_ref[HALO:HALO + tm, :] = cur
    gw = cur.shape[1] // len(POOL_WINDOWS)
    pos1 = lax.broadcasted_iota(I32, (tm, gw), 0) + (i * tm + n_prev + 1)
    ds = []
    for g, w in enumerate(POOL_WINDOWS):
        cs = slice(g * gw, (g + 1) * gw)
        acc = cur[:, cs]
        for sft in range(1, w):
            acc = acc + ext_ref[HALO - sft:HALO - sft + tm, cs]
        cnt = jnp.minimum(pos1, w).astype(F32)
        ds.append(acc / cnt - cur[:, cs])
    d = jnp.concatenate(ds, axis=1)
    o_pool = _dot(d.astype(BF16), wbd_ref[...]) * pscale_ref[...]
    aw = oatt_ref.shape[2]
    o_ref[0] = (x_ref[0] + _dot(oatt_ref[0], wout_ref[0:aw, :])
                + _dot(o_pool.astype(BF16), wout_ref[aw:, :]))


def _outproj(x, o_att, up, halo_arr, wbd, pool_scale, w_out_b, *, n_prev, halo_from_up):
    b, t, d = x.shape
    aw = o_att.shape[2]
    pw = up.shape[2]
    tm = _tile(t, 512)
    if halo_from_up:
        halo_map = lambda bi, i: (bi, jnp.maximum(i * (tm // HALO) - 1, 0), 0)
    else:
        halo_map = lambda bi, i: (bi, 0, 0)
    kern = functools.partial(_outproj_kernel, tm=tm, n_prev=n_prev, zero_first_halo=halo_from_up)
    full = lambda shape: pl.BlockSpec(shape, lambda bi, i: (0,) * len(shape))
    return pl.pallas_call(
        kern,
        grid=(b, t // tm),
        in_specs=[pl.BlockSpec((1, tm, d), lambda bi, i: (bi, i, 0)),
                  pl.BlockSpec((1, tm, aw), lambda bi, i: (bi, i, 0)),
                  pl.BlockSpec((1, tm, pw), lambda bi, i: (bi, i, 0)),
                  pl.BlockSpec((1, HALO, pw), halo_map),
                  full((pw, pw)), full((1, pw)), full((aw + pw, d))],
        out_specs=pl.BlockSpec((1, tm, d), lambda bi, i: (bi, i, 0)),
        out_shape=jax.ShapeDtypeStruct((b, t, d), F32),
        scratch_shapes=[pltpu.VMEM((HALO + tm, pw), F32)],
        compiler_params=_params("parallel", "arbitrary"),
        name="pool_out_proj",
    )(x, o_att, up, halo_arr, wbd, pool_scale, w_out_b)


def _memkv_kernel(mem_ref, gmem_ref, w_ref, gxk_ref, gmat_ref, mk_ref, mv_ref):
    xw = mk_ref.shape[1]
    kv = _dot(_rms(mem_ref[...], gmem_ref[...]).astype(BF16), w_ref[...])
    kx = kv[:, :xw]
    mk_ref[...] = kx * lax.rsqrt(_group_meansq(kx, gmat_ref[...]) + EPS) * gxk_ref[...]
    mv_ref[...] = kv[:, xw:]


def _memkv(mem2d, g_mem, w_xkv_b, gxk, gmat):
    n, d = mem2d.shape
    xw = w_xkv_b.shape[1] // 2
    tm = _tile(n, 512)
    full = lambda shape: pl.BlockSpec(shape, lambda i: (0,) * len(shape))
    return pl.pallas_call(
        _memkv_kernel,
        grid=(n // tm,),
        in_specs=[pl.BlockSpec((tm, d), lambda i: (i, 0)), full((1, d)), full((d, 2 * xw)),
                  full((1, xw)), full((MXU_DIM, MXU_DIM))],
        out_specs=[pl.BlockSpec((tm, xw), lambda i: (i, 0))] * 2,
        out_shape=[jax.ShapeDtypeStruct((n, xw), F32)] * 2,
        compiler_params=_params("parallel"),
        name="mem_kv",
    )(mem2d, g_mem, w_xkv_b, gxk, gmat)


def _xattn_kernel(h_ref, mk_ref, mv_ref, gx_ref, wq_ref, gxq_ref, gmat_ref, wo_ref, o_ref):
    h1 = h_ref[0]
    qx = _dot(_rms(h1, gx_ref[...]).astype(BF16), wq_ref[...])
    qn = (qx * lax.rsqrt(_group_meansq(qx, gmat_ref[...]) + EPS) * gxq_ref[...]).astype(BF16)
    mk = mk_ref[0].astype(BF16)
    mv = mv_ref[0].astype(BF16)
    head_of_lane = lax.broadcasted_iota(I32, qn.shape, 1) // X_HEAD_DIM
    o = jnp.zeros(qn.shape, F32)
    for hh in range(X_HEADS):
        sel = head_of_lane == hh
        s = _dot_nt(jnp.where(sel, qn, jnp.zeros_like(qn)), mk)
        p = jnp.exp(s - jnp.max(s, axis=-1, keepdims=True))
        p = p / jnp.sum(p, axis=-1, keepdims=True)
        o = o + jnp.where(sel, _dot(p.astype(BF16), mv), 0.0)
    o_ref[0] = h1 + _dot(o.astype(BF16), wo_ref[...])


def _xattn(h1, mk, mv, g_xattn, w_xq_b, gxq, gmat, w_xo_b):
    b, t, d = h1.shape
    nm, xw = mk.shape[1], mk.shape[2]
    tq = _tile(t, 512)
    full = lambda shape: pl.BlockSpec(shape, lambda bi, i: (0,) * len(shape))
    return pl.pallas_call(
        _xattn_kernel,
        grid=(b, t // tq),
        in_specs=[pl.BlockSpec((1, tq, d), lambda bi, i: (bi, i, 0)),
                  pl.BlockSpec((1, nm, xw), lambda bi, i: (bi, 0, 0)),
                  pl.BlockSpec((1, nm, xw), lambda bi, i: (bi, 0, 0)),
                  full((1, d)), full((d, xw)), full((1, xw)), full((MXU_DIM, MXU_DIM)), full((xw, d))],
        out_specs=pl.BlockSpec((1, tq, d), lambda bi, i: (bi, i, 0)),
        out_shape=jax.ShapeDtypeStruct((b, t, d), F32),
        compiler_params=_params("parallel", "parallel"),
        name="mem_xattn",
    )(h1, mk, mv, g_xattn, w_xq_b, gxq, gmat, w_xo_b)


def _gather_rows(t, n_experts):
    return -(-(TOP_K * t + n_experts * (SUBLANES - 1)) // LANES) * LANES


def _slot_onehot(pos, n_rows):
    t = pos.shape[0]
    r = lax.broadcasted_iota(I32, (t, n_rows), 1)
    onehot = jnp.zeros((t, n_rows), F32)
    for k in range(TOP_K):
        onehot = jnp.where(r == pos[:, k:k + 1], 1.0, onehot)
    return onehot.astype(BF16)


def _route_kernel(h_ref, gffn_ref, wr_ref, br_ref, xg_ref, pos_ref, cnt_ref, *, n_experts):
    t = h_ref.shape[0]
    n_rows = xg_ref.shape[1]
    mb = _rms(h_ref[...], gffn_ref[...]).astype(BF16)
    logits = _dot(mb, wr_ref[...]) + br_ref[...]
    lane = lax.broadcasted_iota(I32, logits.shape, 1).astype(F32)
    work = logits
    sels, vals = [], []
    for _ in range(TOP_K):
        mx = jnp.max(work, axis=-1, keepdims=True)
        idx = jnp.min(jnp.where(work == mx, lane, float(LANES)), axis=-1, keepdims=True)
        sel = lane == idx
        sels.append(sel)
        vals.append(mx)
        work = jnp.where(sel, 2.0 * NEG, work)
    es = [jnp.exp(v - vals[0]) for v in vals]
    denom = es[0]
    for e in es[1:]:
        denom = denom + e
    hot = jnp.zeros(logits.shape, F32)
    gate_m = jnp.zeros(logits.shape, F32)
    for sel, e in zip(sels, es):
        hot = jnp.where(sel, 1.0, hot)
        gate_m = jnp.where(sel, e / denom, gate_m)
    ri = lax.broadcasted_iota(I32, (t, t), 0)
    ci = lax.broadcasted_iota(I32, (t, t), 1)
    rank = _dot(jnp.where(ci < ri, 1.0, 0.0).astype(BF16), hot.astype(BF16))
    count = jnp.sum(hot, axis=0, keepdims=True).astype(I32)
    c8 = (((count + (SUBLANES - 1)) >> 3) << 3).astype(F32)
    ei = lax.broadcasted_iota(I32, (LANES, LANES), 0)
    ej = lax.broadcasted_iota(I32, (LANES, LANES), 1)
    off = _dot(jnp.broadcast_to(c8, (SUBLANES, LANES)).astype(BF16), jnp.where(ei < ej, 1.0, 0.0).astype(BF16))[0:1]
    slot = off + rank
    lane_i = lax.broadcasted_iota(I32, logits.shape, 1)
    pos = jnp.full(logits.shape, -1, I32)
    for k, sel in enumerate(sels):
        pk = jnp.sum(jnp.where(sel, slot, 0.0), axis=-1, keepdims=True).astype(I32)
        pos = jnp.where(lane_i == k, pk, pos)
    pos_ref[...] = pos
    cnt_ref[0] = jnp.broadcast_to(count, (SUBLANES, LANES))
    g1 = gate_m.astype(BF16)
    r1 = gate_m - g1.astype(F32)
    g2 = r1.astype(BF16)
    g3 = (r1 - g2.astype(F32)).astype(BF16)
    gx = (g1.astype(F32) + pltpu.roll(g2.astype(F32), n_experts, axis=1)
          + pltpu.roll(g3.astype(F32), 2 * n_experts, axis=1)).astype(BF16)
    xg_ref[0] = _dot_tn(_slot_onehot(pos, n_rows), jnp.concatenate([mb, gx], axis=1))


def _route(h2d, g_ffn, wr_pad_b, br_pad, n_experts):
    n, d = h2d.shape
    t = _tile(n, 256)
    n_sub = n // t
    n_rows = _gather_rows(t, n_experts)
    kern = functools.partial(_route_kernel, n_experts=n_experts)
    full = lambda shape: pl.BlockSpec(shape, lambda i: (0,) * len(shape))
    return pl.pallas_call(
        kern,
        grid=(n_sub,),
        in_specs=[pl.BlockSpec((t, d), lambda i: (i, 0)), full((1, d)), full((d, LANES)), full((1, LANES))],
        out_specs=[pl.BlockSpec((1, n_rows, d + LANES), lambda i: (i, 0, 0)),
                   pl.BlockSpec((t, LANES), lambda i: (i, 0)),
                   pl.BlockSpec((1, SUBLANES, LANES), lambda i: (i, 0, 0))],
        out_shape=[jax.ShapeDtypeStruct((n_sub, n_rows, d + LANES), F32),
                   jax.ShapeDtypeStruct((n, LANES), I32),
                   jax.ShapeDtypeStruct((n_sub, SUBLANES, LANES), I32)],
        compiler_params=_params("parallel"),
        name="moe_route_gather",
    )(h2d, g_ffn, wr_pad_b, br_pad)


def _unit_plan(cnt, n_rows, n_experts):
    n_sub = cnt.shape[0]
    upt = n_rows // SUBLANES
    u = (cnt + (SUBLANES - 1)) // SUBLANES
    uoff = jnp.cumsum(u, axis=1) - u
    used = jnp.sum(u, axis=1, keepdims=True)
    u_all = jnp.concatenate([u, upt - used], axis=1)
    uoff_all = jnp.concatenate([uoff, used], axis=1)
    flat_cnt = u_all.T.reshape(-1)
    cum = jnp.cumsum(flat_cnt)
    start = cum - flat_cnt
    k = jnp.arange(upt, dtype=I32)
    grp = jnp.sum(uoff_all[:, None, 1:] <= k[None, :, None], axis=-1)
    shift = start.reshape(n_experts + 1, n_sub).T - uoff_all
    hit = grp[:, :, None] == jnp.arange(n_experts + 1, dtype=I32)[None, None, :]
    dest = k[None, :] + jnp.sum(jnp.where(hit, shift[:, None, :], 0), axis=-1)
    n_units = n_sub * upt
    unit_src = jnp.zeros((n_units,), I32).at[dest.reshape(-1)].set(jnp.arange(n_units, dtype=I32))
    ustart = jnp.concatenate([start[::n_sub], cum[-1:]]).astype(I32)
    return ustart, unit_src


def _expert_stream(k, ustart_ref, usrc_ref, xg_hbm, yg_hbm, xbuf, ybuf, in_sem, out_sem, bgu_ref, bd_ref, wgu_b, wd_b,
                   *, n_experts, upb, d_model):
    e = pl.program_id(0)
    u0 = ustart_ref[e]
    u1 = ustart_ref[e + 1]
    dff = wd_b.shape[0]

    def in_copy(j, slot, i):
        return pltpu.make_async_copy(xg_hbm.at[usrc_ref[j]], xbuf.at[slot, pl.ds(i * SUBLANES, SUBLANES)],
                                     in_sem.at[k, slot])

    def out_copy(j, i):
        return pltpu.make_async_copy(ybuf.at[pl.ds(i * SUBLANES, SUBLANES)], yg_hbm.at[usrc_ref[j]], out_sem.at[k])

    def n_valid(blk):
        return jnp.minimum(upb, u1 - (u0 + blk * upb))

    def for_units(blk, fn):
        base = u0 + blk * upb
        nv = n_valid(blk)

        @pl.when(nv == upb)
        def _():
            for i in range(upb):
                fn(base + i, i)

        @pl.when(nv < upb)
        def _():
            def body(i, c):
                fn(base + i, i)
                return c
            lax.fori_loop(0, nv, body, 0)

    @pl.when(e == 0)
    def _():
        xbuf[...] = jnp.zeros(xbuf.shape, F32)

    @pl.when(e < n_experts)
    def _():
        n_blk = (u1 - u0 + upb - 1) // upb

        @pl.when(n_blk > 0)
        def _():
            for_units(0, lambda j, i: in_copy(j, 0, i).start())

        def block(blk, c):
            slot = blk % 2

            @pl.when(blk + 1 < n_blk)
            def _():
                for_units(blk + 1, lambda j, i: in_copy(j, 1 - slot, i).start())

            for_units(blk, lambda j, i: in_copy(j, slot, i).wait())
            x = xbuf[slot]
            gcols = x[:, d_model:]
            lane = lax.broadcasted_iota(I32, gcols.shape, 1)
            own = (lane % n_experts == e) & (lane < 3 * n_experts)
            gate_w = jnp.sum(jnp.where(own, gcols, 0.0), axis=-1, keepdims=True)
            xb = x[:, :d_model].astype(BF16)
            hidden = []
            for c0 in range(0, dff, MXU_DIM):
                gs = slice(c0, c0 + MXU_DIM)
                us = slice(dff + c0, dff + c0 + MXU_DIM)
                gate = jnp.minimum(_dot(xb, wgu_b[:, gs]) + bgu_ref[0, :, gs], SWIGLU_LIMIT)
                up = jnp.clip(_dot(xb, wgu_b[:, us]) + bgu_ref[0, :, us], -SWIGLU_LIMIT, SWIGLU_LIMIT)
                glu = gate * (1.0 / (1.0 + jnp.exp(-SWIGLU_ALPHA * gate)))
                hidden.append(((up + 1.0) * glu).astype(BF16))
            y = (_dot(jnp.concatenate(hidden, axis=1), wd_b[...]) + bd_ref[0]) * gate_w

            @pl.when(blk > 0)
            def _():
                for_units(blk - 1, lambda j, i: out_copy(j, i).wait())

            ybuf[...] = y
            for_units(blk, lambda j, i: out_copy(j, i).start())
            return c

        lax.fori_loop(0, n_blk, block, 0)

        @pl.when(n_blk > 0)
        def _():
            for_units(n_blk - 1, lambda j, i: out_copy(j, i).wait())

    @pl.when(e == n_experts)
    def _():
        ybuf[...] = jnp.zeros(ybuf.shape, F32)

        def start(j, c):
            out_copy(j, 0).start()
            return c

        def wait(j, c):
            out_copy(j, 0).wait()
            return c
        lax.fori_loop(u0, u1, start, 0)
        lax.fori_loop(u0, u1, wait, 0)


def _expert_kernel(*refs, n_experts, upbs, d_model):
    n = len(upbs)
    plans = refs[:2 * n]
    xg_hbms = refs[2 * n:3 * n]
    wgu_ref, bgu_ref, wd_ref, bd_ref = refs[3 * n:3 * n + 4]
    yg_hbms = refs[3 * n + 4:4 * n + 4]
    xbufs = refs[4 * n + 4:5 * n + 4]
    ybufs = refs[5 * n + 4:6 * n + 4]
    wgu_b, wd_b, in_sem, out_sem = refs[6 * n + 4:]

    @pl.when(pl.program_id(0) < n_experts)
    def _():
        rc = 128
        for r in range(0, wgu_b.shape[0], rc):
            wgu_b[r:r + rc, :] = wgu_ref[0, r:r + rc, :].astype(BF16)
        for r in range(0, wd_b.shape[0], rc):
            wd_b[r:r + rc, :] = wd_ref[0, r:r + rc, :].astype(BF16)

    for k in range(n):
        _expert_stream(k, plans[2 * k], plans[2 * k + 1], xg_hbms[k], yg_hbms[k], xbufs[k], ybufs[k], in_sem, out_sem,
                       bgu_ref, bd_ref, wgu_b, wd_b, n_experts=n_experts, upb=upbs[k], d_model=d_model)


def _experts(plans, xgs, trs, w_gu, b_gu, w_d, b_d):
    n = len(xgs)
    n_experts, d_model, dff2 = w_gu.shape
    dff = w_d.shape[1]
    dx = xgs[0].shape[2]
    units = [x.shape[0] * x.shape[1] // SUBLANES for x in xgs]
    kern = functools.partial(_expert_kernel, n_experts=n_experts, upbs=tuple(tr // SUBLANES for tr in trs),
                             d_model=d_model)
    last = n_experts - 1
    wmap = lambda e, *_: (jnp.minimum(e, last), 0, 0)
    grid_spec = pltpu.PrefetchScalarGridSpec(
        num_scalar_prefetch=2 * n,
        grid=(n_experts + 1,),
        in_specs=[pl.BlockSpec(memory_space=pl.ANY)] * n + [
            pl.BlockSpec((1, d_model, dff2), wmap), pl.BlockSpec((1, 1, dff2), wmap),
            pl.BlockSpec((1, dff, d_model), wmap), pl.BlockSpec((1, 1, d_model), wmap)],
        out_specs=[pl.BlockSpec(memory_space=pl.ANY)] * n,
        scratch_shapes=[pltpu.VMEM((2, tr, dx), F32) for tr in trs] + [pltpu.VMEM((tr, d_model), F32) for tr in trs] + [
            pltpu.VMEM((d_model, dff2), BF16),
            pltpu.VMEM((dff, d_model), BF16),
            pltpu.SemaphoreType.DMA((n, 2)),
            pltpu.SemaphoreType.DMA((n,))])
    ygs = pl.pallas_call(
        kern,
        grid_spec=grid_spec,
        out_shape=[jax.ShapeDtypeStruct((u, SUBLANES, d_model), F32) for u in units],
        compiler_params=_params("arbitrary"),
        name="moe_experts",
    )(*[a for plan in plans for a in plan], *[x.reshape(u, SUBLANES, dx) for x, u in zip(xgs, units)],
      w_gu, b_gu.reshape(n_experts, 1, dff2), w_d, b_d.reshape(n_experts, 1, d_model))
    return [y.reshape(x.shape[0], x.shape[1], d_model) for y, x in zip(ygs, xgs)]


def _combine_kernel(yg_ref, pos_ref, h_ref, o_ref):
    onehot = _slot_onehot(pos_ref[...], yg_ref.shape[1])
    o_ref[...] = h_ref[...] + _dot(onehot, yg_ref[0].astype(BF16))


def _combine(yg, pos, h2d):
    n_sub, n_rows, d = yg.shape
    n = h2d.shape[0]
    t = n // n_sub
    return pl.pallas_call(
        _combine_kernel,
        grid=(n_sub,),
        in_specs=[pl.BlockSpec((1, n_rows, d), lambda i: (i, 0, 0)),
                  pl.BlockSpec((t, LANES), lambda i: (i, 0)),
                  pl.BlockSpec((t, d), lambda i: (i, 0))],
        out_specs=pl.BlockSpec((t, d), lambda i: (i, 0)),
        out_shape=jax.ShapeDtypeStruct((n, d), F32),
        compiler_params=_params("parallel"),
        name="moe_combine",
    )(yg, pos, h2d)


def _moe(hs, g_ffn, wr_pad_b, br_pad, w_gu, b_gu, w_d, b_d):
    n_experts = w_gu.shape[0]
    h2ds = [h.reshape(-1, h.shape[-1]) for h in hs]
    routed = [_route(h2d, g_ffn, wr_pad_b, br_pad, n_experts) for h2d in h2ds]
    plans = [_unit_plan(cnt[:, 0, :n_experts], xg.shape[1], n_experts) for xg, _, cnt in routed]
    trs = [512 if h2d.shape[0] >= 4096 else 128 for h2d in h2ds]
    ygs = _experts(plans, [r[0] for r in routed], trs, w_gu, b_gu, w_d, b_d)
    return [_combine(yg, r[1], h2d).reshape(h.shape) for yg, r, h2d, h in zip(ygs, routed, h2ds, hs)]


def kernel(x_prompt, x_sample, mem_prompt, cache_k, cache_v, cache_mem_k, cache_mem_v, state_pool, page_table,
           g_mix, w_in, g_q, g_k, lam, g_sub, w_pool, pool_scale, w_out,
           g_xattn, g_mem, w_xq, w_xkv, g_xq, g_xk, w_xo,
           g_ffn, w_router, b_router, w_gate_up, b_gate_up, w_down, b_down):
    depth = w_in.shape[0]
    b_p, s_p, d = x_prompt.shape
    b_s, t_s, _ = x_sample.shape
    n_mem = mem_prompt.shape[1]
    n_experts = w_router.shape[2]
    assert LANES % n_experts == 0 and 3 * n_experts <= LANES
    assert ATT_WIDTH % MXU_DIM == 0 and (X_HEADS * X_HEAD_DIM) % MXU_DIM == 0

    slopes = jnp.asarray([2.0 ** (-8.0 * (h + 1) / N_HEADS) for h in range(N_HEADS)], F32)
    grp = jnp.arange(MXU_DIM) // HEAD_DIM
    gmat = (grp[:, None] == grp[None, :]).astype(BF16)
    row = lambda a: a.reshape(1, -1).astype(F32)

    hp, hs = x_prompt, x_sample
    outs = [[] for _ in range(8)]
    for l in range(depth):
        lam_init = 0.8 - 0.6 * math.exp(-0.3 * l)
        w_in_b = w_in[l].astype(BF16)
        w_out_b = w_out[l].astype(BF16)
        w_xq_b = w_xq[l].astype(BF16)
        w_xkv_b = w_xkv[l].astype(BF16)
        w_xo_b = w_xo[l].astype(BF16)
        gqk = jnp.concatenate([jnp.tile(g_q[l].reshape(-1), N_HEADS), jnp.tile(g_k[l].reshape(-1), N_HEADS)])[None, :]
        gxq = (jnp.tile(g_xq[l], X_HEADS) * (X_HEAD_DIM ** -0.5))[None, :]
        gxk = jnp.tile(g_xk[l], X_HEADS)[None, :]
        wbd = jax.scipy.linalg.block_diag(*[w_pool[l, g] for g in range(len(POOL_WINDOWS))]).astype(BF16)
        wr_pad_b = jnp.pad(w_router[l], ((0, 0), (0, LANES - n_experts))).astype(BF16)
        br_pad = jnp.pad(b_router[l].astype(F32), (0, LANES - n_experts), constant_values=NEG)[None, :]
        moe_w = (row(g_ffn[l]), wr_pad_b, br_pad, w_gate_up[l], b_gate_up[l], w_down[l], b_down[l])
        gsub = row(g_sub[l])

        q_b, k_b, v_t, k32, v32, up = _proj(hp.reshape(b_p * s_p, d), row(g_mix[l]), w_in_b, gqk, gmat)
        o_att = _attn(q_b, k_b, v_t, slopes, lam[l], g_sub[l].reshape(-1, 1).astype(F32), s_p, lam_init)
        up3 = up.reshape(b_p, s_p, -1)
        h1 = _outproj(hp, o_att.reshape(b_p, s_p, -1), up3, up3, wbd, row(pool_scale[l]), w_out_b, n_prev=0,
                      halo_from_up=True)
        mk, mv = _memkv(mem_prompt.reshape(b_p * n_mem, d), row(g_mem[l]), w_xkv_b, gxk, gmat)
        h2 = _xattn(h1, mk.reshape(b_p, n_mem, -1), mv.reshape(b_p, n_mem, -1), row(g_xattn[l]), w_xq_b, gxq, gmat,
                    w_xo_b)
        outs[0].append(k32.reshape(b_p, s_p, N_HEADS, V_DIM))
        outs[1].append(v32.reshape(b_p, s_p, N_HEADS, V_DIM))
        outs[2].append(up3[:, s_p - POOL_BUF:])
        outs[3].append(mk.reshape(b_p, n_mem, X_HEADS, X_HEAD_DIM))
        outs[4].append(mv.reshape(b_p, n_mem, X_HEADS, X_HEAD_DIM))

        q_s, _, _, k32_s, v32_s, up_s = _proj(hs.reshape(b_s * t_s, d), row(g_mix[l]), w_in_b, gqk, gmat)
        kn = k32_s.reshape(b_s, t_s, N_HEADS, V_DIM)
        vn = v32_s.reshape(b_s, t_s, N_HEADS, V_DIM)
        o_att_s = _sample_attn(q_s.reshape(b_s, t_s, -1), kn, vn, cache_k, cache_v, l, page_table, slopes, lam[l],
                               gsub, lam_init)
        ups = up_s.reshape(b_s, t_s, -1)
        halo_s = jnp.pad(state_pool[l], ((0, 0), (HALO - POOL_BUF, 0), (0, 0)))
        h1s = _outproj(hs, o_att_s, ups, halo_s, wbd, row(pool_scale[l]), w_out_b, n_prev=POOL_BUF, halo_from_up=False)
        h2s = _xattn(h1s, cache_mem_k[l].reshape(b_s, n_mem, -1), cache_mem_v[l].reshape(b_s, n_mem, -1),
                     row(g_xattn[l]), w_xq_b, gxq, gmat, w_xo_b)
        hp, hs = _moe([h2, h2s], *moe_w)
        outs[5].append(kn.reshape(b_s, t_s, N_HEADS, V_DIM))
        outs[6].append(vn.reshape(b_s, t_s, N_HEADS, V_DIM))
        outs[7].append(jnp.concatenate([state_pool[l], ups], axis=1)[:, t_s:])

    stacked = [jnp.stack(o) for o in outs]
    return (hp, hs, *stacked)
```

```python
import functools
import math

import jax
import jax.numpy as jnp
from jax import lax
from jax.experimental import pallas as pl
from jax.experimental.pallas import tpu as pltpu

F32 = jnp.float32
BF16 = jnp.bfloat16
I32 = jnp.int32

EPS = 1e-6
N_HEADS = 4
HEAD_DIM = 64
V_DIM = 2 * HEAD_DIM
ATT_WIDTH = N_HEADS * V_DIM
POOL_WINDOWS = (2, 4, 8, 16)
POOL_BUF = max(POOL_WINDOWS) - 1
HALO = POOL_BUF + 1
X_HEADS = 4
X_HEAD_DIM = 64
TOP_K = 4
SWIGLU_LIMIT = 7.0
SWIGLU_ALPHA = 1.702
NEG = -1e30

LANES = 128
SUBLANES = 8
MXU_DIM = 256
VMEM_LIMIT = 56 * 1024 * 1024


def _params(*sem):
    return pltpu.CompilerParams(dimension_semantics=sem, vmem_limit_bytes=VMEM_LIMIT)


def _tile(n, pref):
    return pref if n % pref == 0 else n


def _rms(x, g):
    return x * lax.rsqrt(jnp.mean(x * x, axis=-1, keepdims=True) + EPS) * g


def _dot(a, b):
    return jnp.dot(a, b, preferred_element_type=F32)


def _dot_nt(a, b):
    return lax.dot_general(a, b, (((1,), (1,)), ((), ())), preferred_element_type=F32)


def _dot_tn(a, b):
    return lax.dot_general(a, b, (((0,), (0,)), ((), ())), preferred_element_type=F32)


def _group_meansq(u, gmat):
    sq = u * u
    hi = sq.astype(BF16)
    lo = (sq - hi.astype(F32)).astype(BF16)
    outs = []
    for c in range(u.shape[1] // MXU_DIM):
        sl = slice(c * MXU_DIM, (c + 1) * MXU_DIM)
        outs.append(_dot(hi[:, sl], gmat) + _dot(lo[:, sl], gmat))
    return jnp.concatenate(outs, axis=1) * (1.0 / HEAD_DIM)


def _proj_kernel(x_ref, gmix_ref, w_ref, gqk_ref, gmat_ref, q_ref, kb_ref, vt_ref, k_ref, v_ref, up_ref):
    aw = ATT_WIDTH
    xn = _rms(x_ref[...], gmix_ref[...])
    u = _dot(xn.astype(BF16), w_ref[...])
    qk = u[:, :2 * aw]
    qkn = qk * lax.rsqrt(_group_meansq(qk, gmat_ref[...]) + EPS) * gqk_ref[...]
    kn = qkn[:, aw:]
    v = u[:, 2 * aw:3 * aw]
    for h in range(N_HEADS):
        hs = slice(h * V_DIM, (h + 1) * V_DIM)
        k_ref[:, h, :] = kn[:, hs]
        v_ref[:, h, :] = v[:, hs]
    up_ref[...] = u[:, 3 * aw:]
    q_ref[...] = (qkn[:, :aw] * (HEAD_DIM ** -0.5)).astype(BF16)
    kb_ref[...] = kn.astype(BF16)
    vt_ref[...] = v.T.astype(BF16)


def _proj(x2d, g_mix, w_in_b, gqk, gmat):
    n, d = x2d.shape
    wtot = w_in_b.shape[1]
    aw = ATT_WIDTH
    tm = _tile(n, 512)
    full = lambda shape: pl.BlockSpec(shape, lambda i: (0,) * len(shape))
    rows = lambda w: pl.BlockSpec((tm, w), lambda i: (i, 0))
    heads = pl.BlockSpec((tm, N_HEADS, V_DIM), lambda i: (i, 0, 0))
    return pl.pallas_call(
        _proj_kernel,
        grid=(n // tm,),
        in_specs=[rows(d), full((1, d)), full((d, wtot)), full((1, 2 * aw)), full((MXU_DIM, MXU_DIM))],
        out_specs=[rows(aw), rows(aw), pl.BlockSpec((aw, tm), lambda i: (0, i)), heads, heads, rows(wtot - 3 * aw)],
        out_shape=[jax.ShapeDtypeStruct((n, aw), BF16),
                   jax.ShapeDtypeStruct((n, aw), BF16),
                   jax.ShapeDtypeStruct((aw, n), BF16),
                   jax.ShapeDtypeStruct((n, N_HEADS, V_DIM), F32),
                   jax.ShapeDtypeStruct((n, N_HEADS, V_DIM), F32),
                   jax.ShapeDtypeStruct((n, wtot - 3 * aw), F32)],
        compiler_params=_params("parallel"),
        name="in_proj",
    )(x2d, g_mix, w_in_b, gqk, gmat)


def _lambda_value(lam_ref, lam_init):
    lf = lam_ref[...]
    a = jnp.sum(lf[0:1] * lf[1:2], axis=-1, keepdims=True)
    b = jnp.sum(lf[2:3] * lf[3:4], axis=-1, keepdims=True)
    return jnp.exp(a) - jnp.exp(b) + lam_init


def _sub_norm(od, gsub, lam_init):
    return _rms(od, gsub) * (1.0 - lam_init)


def _softmax_step(s, vb, m, l, acc):
    m_new = jnp.maximum(m, jnp.max(s, axis=-1, keepdims=True))
    alpha = jnp.exp(m - m_new)
    p = jnp.exp(s - m_new)
    l = alpha * l + jnp.sum(p, axis=-1, keepdims=True)
    acc = alpha * acc + _dot(p.astype(BF16), vb)
    return m_new, l, acc


POS_SPLIT = 64
ONES_ROWS = 2 * SUBLANES


def _attn_kernel(slopes_ref, lam_ref, gsub_ref, q_ref, k_ref, vt_ref, pf_ref, o_ref, acc_ref, s0_ref, s1_ref,
                 *, t, lam_init):
    slope = slopes_ref[pl.program_id(1)]
    lax.fori_loop(0, q_ref.shape[0] // t,
                  lambda i, c: _attn_query_tile(i, slope, lam_ref, gsub_ref, q_ref, k_ref, vt_ref, pf_ref, o_ref, acc_ref,
                                                s0_ref, s1_ref, t=t, lam_init=lam_init), 0)


def _attn_query_tile(i, slope, lam_ref, gsub_ref, q_ref, k_ref, vt_ref, pf_ref, o_ref, acc_ref, s0_ref, s1_ref,
                     *, t, lam_init):
    q0 = pl.multiple_of(i * t, t)
    q = q_ref[pl.ds(q0, t), :]
    lane = lax.broadcasted_iota(I32, q.shape, 1)
    zero = jnp.zeros_like(q)
    feat = jnp.where(lane == 0, slope * POS_SPLIT, jnp.where(lane == 1, slope, 0.0)).astype(BF16)
    qa = (jnp.concatenate([jnp.where(lane < HEAD_DIM, q, zero), feat], axis=1),
          jnp.concatenate([jnp.where(lane >= HEAD_DIM, q, zero), feat], axis=1))
    acc_ref[...] = jnp.zeros(acc_ref.shape, F32)
    s_bufs = (s0_ref, s1_ref)

    def scores(j, buf):
        k0 = pl.multiple_of(j * t, t)
        kaug = jnp.concatenate([k_ref[pl.ds(k0, t), :], pf_ref[pl.ds(k0, t), :]], axis=1)
        for c in range(2):
            s_bufs[buf][c] = _dot_nt(kaug, qa[c])

    def absorb(j, buf, ms, masked):
        k0 = pl.multiple_of(j * t, t)
        vt = jnp.concatenate([vt_ref[:, pl.ds(k0, t)], jnp.ones((ONES_ROWS, t), BF16)], axis=0)
        out = []
        for c in range(2):
            s = s_bufs[buf][c]
            if masked:
                kr = lax.broadcasted_iota(I32, s.shape, 0)
                qc = lax.broadcasted_iota(I32, s.shape, 1)
                s = jnp.where(kr <= qc, s, NEG)
            m_new = jnp.maximum(ms[c], jnp.max(s, axis=0, keepdims=True))
            alpha = jnp.exp(ms[c] - m_new)
            acc_ref[c] = alpha * acc_ref[c] + _dot(vt, jnp.exp(s - m_new).astype(BF16))
            out.append(m_new)
        return tuple(out)

    def pair(jj, ms):
        j = 2 * jj
        scores(j + 1, 1)
        ms = absorb(j, 0, ms, False)
        scores(j + 2, 0)
        return absorb(j + 1, 1, ms, False)

    scores(0, 0)
    ms = lax.fori_loop(0, i // 2, pair, (jnp.full((1, t), NEG, F32),) * 2)

    @pl.when(i % 2 == 0)
    def _():
        absorb(i, 0, ms, True)

    @pl.when(i % 2 == 1)
    def _():
        scores(i, 1)
        absorb(i, 1, absorb(i - 1, 0, ms, False), True)

    o1 = acc_ref[0, :V_DIM, :] / acc_ref[0, V_DIM:V_DIM + 1, :]
    o2 = acc_ref[1, :V_DIM, :] / acc_ref[1, V_DIM:V_DIM + 1, :]
    od = o1 - _lambda_value(lam_ref, lam_init) * o2
    y = od * lax.rsqrt(jnp.mean(od * od, axis=0, keepdims=True) + EPS) * gsub_ref[...] * (1.0 - lam_init)
    o_ref[pl.ds(q0, t), :] = y.T.astype(o_ref.dtype)
    return 0


def _attn(q, kb, vt, slopes, lam, gsub_col, seq, lam_init):
    n = q.shape[0]
    b = n // seq
    t = _tile(seq, 512)
    assert seq <= POS_SPLIT * 256, "key positions must split into two bf16-exact factors"
    pos = jnp.arange(seq)
    pf = jnp.zeros((seq, V_DIM), F32).at[:, 0].set(pos // POS_SPLIT).at[:, 1].set(pos % POS_SPLIT).astype(BF16)
    kern = functools.partial(_attn_kernel, t=t, lam_init=lam_init)
    return pl.pallas_call(
        kern,
        grid=(b, N_HEADS),
        in_specs=[pl.BlockSpec(memory_space=pltpu.SMEM),
                  pl.BlockSpec(lam.shape, lambda bi, h: (0, 0)),
                  pl.BlockSpec((V_DIM, 1), lambda bi, h: (0, 0)),
                  pl.BlockSpec((seq, V_DIM), lambda bi, h: (bi, h)),
                  pl.BlockSpec((seq, V_DIM), lambda bi, h: (bi, h)),
                  pl.BlockSpec((V_DIM, seq), lambda bi, h: (h, bi)),
                  pl.BlockSpec((seq, V_DIM), lambda bi, h: (0, 0))],
        out_specs=pl.BlockSpec((seq, V_DIM), lambda bi, h: (bi, h)),
        out_shape=jax.ShapeDtypeStruct((n, ATT_WIDTH), BF16),
        scratch_shapes=[pltpu.VMEM((2, V_DIM + ONES_ROWS, t), F32),
                        pltpu.VMEM((2, t, t), F32),
                        pltpu.VMEM((2, t, t), F32)],
        compiler_params=_params("parallel", "parallel"),
        name="prompt_attn",
    )(slopes, lam, gsub_col, q, kb, vt, pf)


def _page_copies(pt_ref, ck_hbm, cv_hbm, kbuf, vbuf, sem, step, slot, *, layer, n_chunks, pages_per_chunk, page):
    b = step // n_chunks
    c = step % n_chunks
    n_pages = n_chunks * pages_per_chunk
    copies = []
    for p in range(pages_per_chunk):
        phys = pt_ref[b * n_pages + c * pages_per_chunk + p]
        dst = pl.ds(p * page, page)
        copies.append(pltpu.make_async_copy(ck_hbm.at[layer, phys], kbuf.at[slot, dst], sem.at[0, slot]))
        copies.append(pltpu.make_async_copy(cv_hbm.at[layer, phys], vbuf.at[slot, dst], sem.at[1, slot]))
    return copies


def _sample_attn_kernel(pt_ref, slopes_ref, lam_ref, gsub_ref, q_ref, kn_ref, vn_ref, ck_hbm, cv_hbm, o_ref,
                        kbuf, vbuf, sem, m_ref, l_ref, acc_ref, bias_ref, *, layer, n_chunks, pages_per_chunk, page,
                        t_new, lam_init):
    b = pl.program_id(0)
    c = pl.program_id(1)
    step = b * n_chunks + c
    n_steps = pl.num_programs(0) * n_chunks
    slot = step % 2
    copies = functools.partial(_page_copies, pt_ref, ck_hbm, cv_hbm, kbuf, vbuf, sem, layer=layer,
                               n_chunks=n_chunks, pages_per_chunk=pages_per_chunk, page=page)

    def start_all(cps):
        for n, cp in enumerate(cps):
            cp.start(priority=n % 2)

    @pl.when(step == 0)
    def _():
        start_all(copies(step, slot))

    @pl.when(step + 1 < n_steps)
    def _():
        start_all(copies(step + 1, 1 - slot))

    @pl.when(c == 0)
    def _():
        m_ref[...] = jnp.full(m_ref.shape, NEG, F32)
        l_ref[...] = jnp.zeros(l_ref.shape, F32)
        acc_ref[...] = jnp.zeros(acc_ref.shape, F32)

    qh = q_ref[0]
    n_rows = qh.shape[0]
    ck = pages_per_chunk * page
    keys = ck // N_HEADS
    past = n_chunks * keys
    rows = lax.broadcasted_iota(I32, (n_rows, 1), 0)
    tok = rows % t_new
    head = rows // (2 * t_new)
    slope = jnp.zeros((n_rows, 1), F32)
    for hh in range(N_HEADS):
        slope = jnp.where(head == hh, slopes_ref[hh], slope)

    @pl.when(step == 0)
    def _():
        cols = lax.broadcasted_iota(I32, (n_rows, ck), 1)
        bias_ref[...] = jnp.where(cols % N_HEADS == head, slope * (cols // N_HEADS).astype(F32), NEG)

    for cp in copies(step, slot):
        cp.wait()

    kb = kbuf[slot].astype(BF16)
    vb = vbuf[slot].astype(BF16)
    s = _dot_nt(qh, kb) + bias_ref[...] + slope * (c * keys - past - tok).astype(F32)
    m, l, acc = _softmax_step(s, vb, m_ref[...], l_ref[...], acc_ref[...])
    m_ref[...] = m
    l_ref[...] = l
    acc_ref[...] = acc

    @pl.when(c == n_chunks - 1)
    def _():
        knb = kn_ref[0].astype(BF16)
        vnb = vn_ref[0].astype(BF16)
        ncol = lax.broadcasted_iota(I32, (n_rows, knb.shape[0]), 1)
        nkey = ncol // N_HEADS
        nd = tok - nkey
        sn = _dot_nt(qh, knb) - slope * nd.astype(F32)
        sn = jnp.where((nd >= 0) & (nkey < t_new) & (ncol % N_HEADS == head), sn, NEG)
        _, l2, acc2 = _softmax_step(sn, vnb, m, l, acc)
        o = acc2 / l2
        od = o - _lambda_value(lam_ref, lam_init) * pltpu.roll(o, n_rows - t_new, axis=0)
        o_ref[0] = _sub_norm(od, gsub_ref[...], lam_init).astype(o_ref.dtype)


def _sample_attn(q, kn, vn, cache_k, cache_v, layer, page_table, slopes, lam, gsub, lam_init):
    b, t_new, width = q.shape
    depth, n_phys, page = cache_k.shape[:3]
    n_pages = page_table.shape[1]
    ppc = 32 if n_pages % 32 == 0 else n_pages
    n_chunks = n_pages // ppc
    n_rows = N_HEADS * 2 * t_new
    pairs = page * N_HEADS
    qh = q.reshape(b, t_new, N_HEADS, V_DIM).transpose(0, 2, 1, 3)
    map_mask = (jnp.arange(V_DIM)[None, :] // HEAD_DIM == jnp.arange(2)[:, None]).astype(q.dtype)
    qh = (qh[:, :, None] * map_mask[None, None, :, None, :]).reshape(b, n_rows, V_DIM)
    new_rows = lambda a: jnp.pad(a, ((0, 0), (0, SUBLANES - t_new), (0, 0), (0, 0))).reshape(b, -1, V_DIM)
    kern = functools.partial(_sample_attn_kernel, layer=layer, n_chunks=n_chunks, pages_per_chunk=ppc, page=pairs,
                             t_new=t_new, lam_init=lam_init)
    grid_spec = pltpu.PrefetchScalarGridSpec(
        num_scalar_prefetch=1,
        grid=(b, n_chunks),
        in_specs=[pl.BlockSpec(memory_space=pltpu.SMEM),
                  pl.BlockSpec(lam.shape, lambda bi, c, pt: (0, 0)),
                  pl.BlockSpec((1, V_DIM), lambda bi, c, pt: (0, 0)),
                  pl.BlockSpec((1, n_rows, V_DIM), lambda bi, c, pt: (bi, 0, 0)),
                  pl.BlockSpec((1, SUBLANES * N_HEADS, V_DIM), lambda bi, c, pt: (bi, 0, 0)),
                  pl.BlockSpec((1, SUBLANES * N_HEADS, V_DIM), lambda bi, c, pt: (bi, 0, 0)),
                  pl.BlockSpec(memory_space=pl.ANY),
                  pl.BlockSpec(memory_space=pl.ANY)],
        out_specs=pl.BlockSpec((1, n_rows, V_DIM), lambda bi, c, pt: (bi, 0, 0)),
        scratch_shapes=[pltpu.VMEM((2, ppc * pairs, V_DIM), F32),
                        pltpu.VMEM((2, ppc * pairs, V_DIM), F32),
                        pltpu.SemaphoreType.DMA((2, 2)),
                        pltpu.VMEM((n_rows, 1), F32),
                        pltpu.VMEM((n_rows, 1), F32),
                        pltpu.VMEM((n_rows, V_DIM), F32),
                        pltpu.VMEM((n_rows, ppc * pairs), F32)])
    out = pl.pallas_call(
        kern,
        grid_spec=grid_spec,
        out_shape=jax.ShapeDtypeStruct((b, n_rows, V_DIM), BF16),
        compiler_params=_params("arbitrary", "arbitrary"),
        name="sample_attn",
    )(page_table.reshape(-1), slopes, lam, gsub, qh, new_rows(kn), new_rows(vn),
      cache_k.reshape(depth, n_phys, pairs, V_DIM), cache_v.reshape(depth, n_phys, pairs, V_DIM))
    out = out.reshape(b, N_HEADS, 2, t_new, V_DIM)[:, :, 0]
    return out.transpose(0, 2, 1, 3).reshape(b, t_new, width)


def _outproj_kernel(x_ref, oatt_ref, up_ref, halo_ref, wbd_ref, pscale_ref, wout_ref, o_ref, ext_ref, *,
                    tm, n_prev, zero_first_halo):
    i = pl.program_id(1)
    halo = halo_ref[0]
    if zero_first_halo:
        halo = jnp.where(i == 0, 0.0, halo)
    cur = up_ref[0]
    ext_ref[0:HALO, :] = halo
    ext_ref[HALO:HALO + tm, :] = cur
    gw = cur.shape[1] // len(POOL_WINDOWS)
    pos1 = lax.broadcasted_iota(I32, (tm, gw), 0) + (i * tm + n_prev + 1)
    ds = []
    for g, w in enumerate(POOL_WINDOWS):
        cs = slice(g * gw, (g + 1) * gw)
        acc = cur[:, cs]
        for sft in range(1, w):
            acc = acc + ext_ref[HALO - sft:HALO - sft + tm, cs]
        cnt = jnp.minimum(pos1, w).astype(F32)
        ds.append(acc / cnt - cur[:, cs])
    d = jnp.concatenate(ds, axis=1)
    o_pool = _dot(d.astype(BF16), wbd_ref[...]) * pscale_ref[...]
    aw = oatt_ref.shape[2]
    o_ref[0] = (x_ref[0] + _dot(oatt_ref[0], wout_ref[0:aw, :])
                + _dot(o_pool.astype(BF16), wout_ref[aw:, :]))


def _outproj(x, o_att, up, halo_arr, wbd, pool_scale, w_out_b, *, n_prev, halo_from_up):
    b, t, d = x.shape
    aw = o_att.shape[2]
    pw = up.shape[2]
    tm = _tile(t, 512)
    if halo_from_up:
        halo_map = lambda bi, i: (bi, jnp.maximum(i * (tm // HALO) - 1, 0), 0)
    else:
        halo_map = lambda bi, i: (bi, 0, 0)
    kern = functools.partial(_outproj_kernel, tm=tm, n_prev=n_prev, zero_first_halo=halo_from_up)
    full = lambda shape: pl.BlockSpec(shape, lambda bi, i: (0,) * len(shape))
    return pl.pallas_call(
        kern,
        grid=(b, t // tm),
        in_specs=[pl.BlockSpec((1, tm, d), lambda bi, i: (bi, i, 0)),
                  pl.BlockSpec((1, tm, aw), lambda bi, i: (bi, i, 0)),
                  pl.BlockSpec((1, tm, pw), lambda bi, i: (bi, i, 0)),
                  pl.BlockSpec((1, HALO, pw), halo_map),
                  full((pw, pw)), full((1, pw)), full((aw + pw, d))],
        out_specs=pl.BlockSpec((1, tm, d), lambda bi, i: (bi, i, 0)),
        out_shape=jax.ShapeDtypeStruct((b, t, d), F32),
        scratch_shapes=[pltpu.VMEM((HALO + tm, pw), F32)],
        compiler_params=_params("parallel", "arbitrary"),
        name="pool_out_proj",
    )(x, o_att, up, halo_arr, wbd, pool_scale, w_out_b)


def _memkv_kernel(mem_ref, gmem_ref, w_ref, gxk_ref, gmat_ref, mk_ref, mv_ref):
    xw = mk_ref.shape[1]
    kv = _dot(_rms(mem_ref[...], gmem_ref[...]).astype(BF16), w_ref[...])
    kx = kv[:, :xw]
    mk_ref[...] = kx * lax.rsqrt(_group_meansq(kx, gmat_ref[...]) + EPS) * gxk_ref[...]
    mv_ref[...] = kv[:, xw:]


def _memkv(mem2d, g_mem, w_xkv_b, gxk, gmat):
    n, d = mem2d.shape
    xw = w_xkv_b.shape[1] // 2
    tm = _tile(n, 512)
    full = lambda shape: pl.BlockSpec(shape, lambda i: (0,) * len(shape))
    return pl.pallas_call(
        _memkv_kernel,
        grid=(n // tm,),
        in_specs=[pl.BlockSpec((tm, d), lambda i: (i, 0)), full((1, d)), full((d, 2 * xw)),
                  full((1, xw)), full((MXU_DIM, MXU_DIM))],
        out_specs=[pl.BlockSpec((tm, xw), lambda i: (i, 0))] * 2,
        out_shape=[jax.ShapeDtypeStruct((n, xw), F32)] * 2,
        compiler_params=_params("parallel"),
        name="mem_kv",
    )(mem2d, g_mem, w_xkv_b, gxk, gmat)


def _xattn_kernel(h_ref, mk_ref, mv_ref, gx_ref, wq_ref, gxq_ref, gmat_ref, wo_ref, o_ref):
    h1 = h_ref[0]
    qx = _dot(_rms(h1, gx_ref[...]).astype(BF16), wq_ref[...])
    qn = (qx * lax.rsqrt(_group_meansq(qx, gmat_ref[...]) + EPS) * gxq_ref[...]).astype(BF16)
    mk = mk_ref[0].astype(BF16)
    mv = mv_ref[0].astype(BF16)
    head_of_lane = lax.broadcasted_iota(I32, qn.shape, 1) // X_HEAD_DIM
    o = jnp.zeros(qn.shape, F32)
    for hh in range(X_HEADS):
        sel = head_of_lane == hh
        s = _dot_nt(jnp.where(sel, qn, jnp.zeros_like(qn)), mk)
        p = jnp.exp(s - jnp.max(s, axis=-1, keepdims=True))
        p = p / jnp.sum(p, axis=-1, keepdims=True)
        o = o + jnp.where(sel, _dot(p.astype(BF16), mv), 0.0)
    o_ref[0] = h1 + _dot(o.astype(BF16), wo_ref[...])


def _xattn(h1, mk, mv, g_xattn, w_xq_b, gxq, gmat, w_xo_b):
    b, t, d = h1.shape
    nm, xw = mk.shape[1], mk.shape[2]
    tq = _tile(t, 512)
    full = lambda shape: pl.BlockSpec(shape, lambda bi, i: (0,) * len(shape))
    return pl.pallas_call(
        _xattn_kernel,
        grid=(b, t // tq),
        in_specs=[pl.BlockSpec((1, tq, d), lambda bi, i: (bi, i, 0)),
                  pl.BlockSpec((1, nm, xw), lambda bi, i: (bi, 0, 0)),
                  pl.BlockSpec((1, nm, xw), lambda bi, i: (bi, 0, 0)),
                  full((1, d)), full((d, xw)), full((1, xw)), full((MXU_DIM, MXU_DIM)), full((xw, d))],
        out_specs=pl.BlockSpec((1, tq, d), lambda bi, i: (bi, i, 0)),
        out_shape=jax.ShapeDtypeStruct((b, t, d), F32),
        compiler_params=_params("parallel", "parallel"),
        name="mem_xattn",
    )(h1, mk, mv, g_xattn, w_xq_b, gxq, gmat, w_xo_b)


def _gather_rows(t, n_experts):
    return -(-(TOP_K * t + n_experts * (SUBLANES - 1)) // LANES) * LANES


def _slot_onehot(pos, n_rows):
    t = pos.shape[0]
    r = lax.broadcasted_iota(I32, (t, n_rows), 1)
    onehot = jnp.zeros((t, n_rows), F32)
    for k in range(TOP_K):
        onehot = jnp.where(r == pos[:, k:k + 1], 1.0, onehot)
    return onehot.astype(BF16)


def _route_kernel(h_ref, gffn_ref, wr_ref, br_ref, xg_ref, pos_ref, cnt_ref, *, n_experts):
    t = h_ref.shape[0]
    n_rows = xg_ref.shape[1]
    mb = _rms(h_ref[...], gffn_ref[...]).astype(BF16)
    logits = _dot(mb, wr_ref[...]) + br_ref[...]
    lane = lax.broadcasted_iota(I32, logits.shape, 1).astype(F32)
    work = logits
    sels, vals = [], []
    for _ in range(TOP_K):
        mx = jnp.max(work, axis=-1, keepdims=True)
        idx = jnp.min(jnp.where(work == mx, lane, float(LANES)), axis=-1, keepdims=True)
        sel = lane == idx
        sels.append(sel)
        vals.append(mx)
        work = jnp.where(sel, 2.0 * NEG, work)
    es = [jnp.exp(v - vals[0]) for v in vals]
    denom = es[0]
    for e in es[1:]:
        denom = denom + e
    hot = jnp.zeros(logits.shape, F32)
    gate_m = jnp.zeros(logits.shape, F32)
    for sel, e in zip(sels, es):
        hot = jnp.where(sel, 1.0, hot)
        gate_m = jnp.where(sel, e / denom, gate_m)
    ri = lax.broadcasted_iota(I32, (t, t), 0)
    ci = lax.broadcasted_iota(I32, (t, t), 1)
    rank = _dot(jnp.where(ci < ri, 1.0, 0.0).astype(BF16), hot.astype(BF16))
    count = jnp.sum(hot, axis=0, keepdims=True).astype(I32)
    c8 = (((count + (SUBLANES - 1)) >> 3) << 3).astype(F32)
    ei = lax.broadcasted_iota(I32, (LANES, LANES), 0)
    ej = lax.broadcasted_iota(I32, (LANES, LANES), 1)
    off = _dot(jnp.broadcast_to(c8, (SUBLANES, LANES)).astype(BF16), jnp.where(ei < ej, 1.0, 0.0).astype(BF16))[0:1]
    slot = off + rank
    lane_i = lax.broadcasted_iota(I32, logits.shape, 1)
    pos = jnp.full(logits.shape, -1, I32)
    for k, sel in enumerate(sels):
        pk = jnp.sum(jnp.where(sel, slot, 0.0), axis=-1, keepdims=True).astype(I32)
        pos = jnp.where(lane_i == k, pk, pos)
    pos_ref[...] = pos
    cnt_ref[0] = jnp.broadcast_to(count, (SUBLANES, LANES))
    g1 = gate_m.astype(BF16)
    r1 = gate_m - g1.astype(F32)
    g2 = r1.astype(BF16)
    g3 = (r1 - g2.astype(F32)).astype(BF16)
    gx = (g1.astype(F32) + pltpu.roll(g2.astype(F32), n_experts, axis=1)
          + pltpu.roll(g3.astype(F32), 2 * n_experts, axis=1)).astype(BF16)
    xg_ref[0] = _dot_tn(_slot_onehot(pos, n_rows), jnp.concatenate([mb, gx], axis=1))


def _route(h2d, g_ffn, wr_pad_b, br_pad, n_experts):
    n, d = h2d.shape
    t = _tile(n, 256)
    n_sub = n // t
    n_rows = _gather_rows(t, n_experts)
    kern = functools.partial(_route_kernel, n_experts=n_experts)
    full = lambda shape: pl.BlockSpec(shape, lambda i: (0,) * len(shape))
    return pl.pallas_call(
        kern,
        grid=(n_sub,),
        in_specs=[pl.BlockSpec((t, d), lambda i: (i, 0)), full((1, d)), full((d, LANES)), full((1, LANES))],
        out_specs=[pl.BlockSpec((1, n_rows, d + LANES), lambda i: (i, 0, 0)),
                   pl.BlockSpec((t, LANES), lambda i: (i, 0)),
                   pl.BlockSpec((1, SUBLANES, LANES), lambda i: (i, 0, 0))],
        out_shape=[jax.ShapeDtypeStruct((n_sub, n_rows, d + LANES), F32),
                   jax.ShapeDtypeStruct((n, LANES), I32),
                   jax.ShapeDtypeStruct((n_sub, SUBLANES, LANES), I32)],
        compiler_params=_params("parallel"),
        name="moe_route_gather",
    )(h2d, g_ffn, wr_pad_b, br_pad)


def _unit_plan(cnt, n_rows, n_experts):
    n_sub = cnt.shape[0]
    upt = n_rows // SUBLANES
    u = (cnt + (SUBLANES - 1)) // SUBLANES
    uoff = jnp.cumsum(u, axis=1) - u
    used = jnp.sum(u, axis=1, keepdims=True)
    u_all = jnp.concatenate([u, upt - used], axis=1)
    uoff_all = jnp.concatenate([uoff, used], axis=1)
    flat_cnt = u_all.T.reshape(-1)
    cum = jnp.cumsum(flat_cnt)
    start = cum - flat_cnt
    k = jnp.arange(upt, dtype=I32)
    grp = jnp.sum(uoff_all[:, None, 1:] <= k[None, :, None], axis=-1)
    shift = start.reshape(n_experts + 1, n_sub).T - uoff_all
    hit = grp[:, :, None] == jnp.arange(n_experts + 1, dtype=I32)[None, None, :]
    dest = k[None, :] + jnp.sum(jnp.where(hit, shift[:, None, :], 0), axis=-1)
    n_units = n_sub * upt
    unit_src = jnp.zeros((n_units,), I32).at[dest.reshape(-1)].set(jnp.arange(n_units, dtype=I32))
    ustart = jnp.concatenate([start[::n_sub], cum[-1:]]).astype(I32)
    return ustart, unit_src


def _expert_stream(k, ustart_ref, usrc_ref, xg_hbm, yg_hbm, xbuf, ybuf, in_sem, out_sem, bgu_ref, bd_ref, wgu_b, wd_b,
                   *, n_experts, upb, d_model):
    e = pl.program_id(0)
    u0 = ustart_ref[e]
    u1 = ustart_ref[e + 1]
    dff = wd_b.shape[0]

    def in_copy(j, slot, i):
        return pltpu.make_async_copy(xg_hbm.at[usrc_ref[j]], xbuf.at[slot, pl.ds(i * SUBLANES, SUBLANES)],
                                     in_sem.at[k, slot])

    def out_copy(j, i):
        return pltpu.make_async_copy(ybuf.at[pl.ds(i * SUBLANES, SUBLANES)], yg_hbm.at[usrc_ref[j]], out_sem.at[k])

    def n_valid(blk):
        return jnp.minimum(upb, u1 - (u0 + blk * upb))

    def for_units(blk, fn):
        base = u0 + blk * upb
        nv = n_valid(blk)

        @pl.when(nv == upb)
        def _():
            for i in range(upb):
                fn(base + i, i)

        @pl.when(nv < upb)
        def _():
            def body(i, c):
                fn(base + i, i)
                return c
            lax.fori_loop(0, nv, body, 0)

    @pl.when(e == 0)
    def _():
        xbuf[...] = jnp.zeros(xbuf.shape, F32)

    @pl.when(e < n_experts)
    def _():
        n_blk = (u1 - u0 + upb - 1) // upb

        @pl.when(n_blk > 0)
        def _():
            for_units(0, lambda j, i: in_copy(j, 0, i).start())

        def block(blk, c):
            slot = blk % 2

            @pl.when(blk + 1 < n_blk)
            def _():
                for_units(blk + 1, lambda j, i: in_copy(j, 1 - slot, i).start())

            for_units(blk, lambda j, i: in_copy(j, slot, i).wait())
            x = xbuf[slot]
            gcols = x[:, d_model:]
            lane = lax.broadcasted_iota(I32, gcols.shape, 1)
            own = (lane % n_experts == e) & (lane < 3 * n_experts)
            gate_w = jnp.sum(jnp.where(own, gcols, 0.0), axis=-1, keepdims=True)
            xb = x[:, :d_model].astype(BF16)
            hidden = []
            for c0 in range(0, dff, MXU_DIM):
                gs = slice(c0, c0 + MXU_DIM)
                us = slice(dff + c0, dff + c0 + MXU_DIM)
                gate = jnp.minimum(_dot(xb, wgu_b[:, gs]) + bgu_ref[0, :, gs], SWIGLU_LIMIT)
                up = jnp.clip(_dot(xb, wgu_b[:, us]) + bgu_ref[0, :, us], -SWIGLU_LIMIT, SWIGLU_LIMIT)
                glu = gate * (1.0 / (1.0 + jnp.exp(-SWIGLU_ALPHA * gate)))
                hidden.append(((up + 1.0) * glu).astype(BF16))
            y = (_dot(jnp.concatenate(hidden, axis=1), wd_b[...]) + bd_ref[0]) * gate_w

            @pl.when(blk > 0)
            def _():
                for_units(blk - 1, lambda j, i: out_copy(j, i).wait())

            ybuf[...] = y
            for_units(blk, lambda j, i: out_copy(j, i).start())
            return c

        lax.fori_loop(0, n_blk, block, 0)

        @pl.when(n_blk > 0)
        def _():
            for_units(n_blk - 1, lambda j, i: out_copy(j, i).wait())

    @pl.when(e == n_experts)
    def _():
        ybuf[...] = jnp.zeros(ybuf.shape, F32)

        def start(j, c):
            out_copy(j, 0).start()
            return c

        def wait(j, c):
            out_copy(j, 0).wait()
            return c
        lax.fori_loop(u0, u1, start, 0)
        lax.fori_loop(u0, u1, wait, 0)


def _expert_kernel(*refs, n_experts, upbs, d_model):
    n = len(upbs)
    plans = refs[:2 * n]
    xg_hbms = refs[2 * n:3 * n]
    wgu_ref, bgu_ref, wd_ref, bd_ref = refs[3 * n:3 * n + 4]
    yg_hbms = refs[3 * n + 4:4 * n + 4]
    xbufs = refs[4 * n + 4:5 * n + 4]
    ybufs = refs[5 * n + 4:6 * n + 4]
    wgu_b, wd_b, in_sem, out_sem = refs[6 * n + 4:]

    @pl.when(pl.program_id(0) < n_experts)
    def _():
        rc = 128
        for r in range(0, wgu_b.shape[0], rc):
            wgu_b[r:r + rc, :] = wgu_ref[0, r:r + rc, :].astype(BF16)
        for r in range(0, wd_b.shape[0], rc):
            wd_b[r:r + rc, :] = wd_ref[0, r:r + rc, :].astype(BF16)

    for k in range(n):
        _expert_stream(k, plans[2 * k], plans[2 * k + 1], xg_hbms[k], yg_hbms[k], xbufs[k], ybufs[k], in_sem, out_sem,
                       bgu_ref, bd_ref, wgu_b, wd_b, n_experts=n_experts, upb=upbs[k], d_model=d_model)


def _experts(plans, xgs, trs, w_gu, b_gu, w_d, b_d):
    n = len(xgs)
    n_experts, d_model, dff2 = w_gu.shape
    dff = w_d.shape[1]
    dx = xgs[0].shape[2]
    units = [x.shape[0] * x.shape[1] // SUBLANES for x in xgs]
    kern = functools.partial(_expert_kernel, n_experts=n_experts, upbs=tuple(tr // SUBLANES for tr in trs),
                             d_model=d_model)
    last = n_experts - 1
    wmap = lambda e, *_: (jnp.minimum(e, last), 0, 0)
    grid_spec = pltpu.PrefetchScalarGridSpec(
        num_scalar_prefetch=2 * n,
        grid=(n_experts + 1,),
        in_specs=[pl.BlockSpec(memory_space=pl.ANY)] * n + [
            pl.BlockSpec((1, d_model, dff2), wmap), pl.BlockSpec((1, 1, dff2), wmap),
            pl.BlockSpec((1, dff, d_model), wmap), pl.BlockSpec((1, 1, d_model), wmap)],
        out_specs=[pl.BlockSpec(memory_space=pl.ANY)] * n,
        scratch_shapes=[pltpu.VMEM((2, tr, dx), F32) for tr in trs] + [pltpu.VMEM((tr, d_model), F32) for tr in trs] + [
            pltpu.VMEM((d_model, dff2), BF16),
            pltpu.VMEM((dff, d_model), BF16),
            pltpu.SemaphoreType.DMA((n, 2)),
            pltpu.SemaphoreType.DMA((n,))])
    ygs = pl.pallas_call(
        kern,
        grid_spec=grid_spec,
        out_shape=[jax.ShapeDtypeStruct((u, SUBLANES, d_model), F32) for u in units],
        compiler_params=_params("arbitrary"),
        name="moe_experts",
    )(*[a for plan in plans for a in plan], *[x.reshape(u, SUBLANES, dx) for x, u in zip(xgs, units)],
      w_gu, b_gu.reshape(n_experts, 1, dff2), w_d, b_d.reshape(n_experts, 1, d_model))
    return [y.reshape(x.shape[0], x.shape[1], d_model) for y, x in zip(ygs, xgs)]


def _combine_kernel(yg_ref, pos_ref, h_ref, o_ref):
    onehot = _slot_onehot(pos_ref[...], yg_ref.shape[1])
    o_ref[...] = h_ref[...] + _dot(onehot, yg_ref[0].astype(BF16))


def _combine(yg, pos, h2d):
    n_sub, n_rows, d = yg.shape
    n = h2d.shape[0]
    t = n // n_sub
    return pl.pallas_call(
        _combine_kernel,
        grid=(n_sub,),
        in_specs=[pl.BlockSpec((1, n_rows, d), lambda i: (i, 0, 0)),
                  pl.BlockSpec((t, LANES), lambda i: (i, 0)),
                  pl.BlockSpec((t, d), lambda i: (i, 0))],
        out_specs=pl.BlockSpec((t, d), lambda i: (i, 0)),
        out_shape=jax.ShapeDtypeStruct((n, d), F32),
        compiler_params=_params("parallel"),
        name="moe_combine",
    )(yg, pos, h2d)


def _moe(hs, g_ffn, wr_pad_b, br_pad, w_gu, b_gu, w_d, b_d):
    n_experts = w_gu.shape[0]
    h2ds = [h.reshape(-1, h.shape[-1]) for h in hs]
    routed = [_route(h2d, g_ffn, wr_pad_b, br_pad, n_experts) for h2d in h2ds]
    plans = [_unit_plan(cnt[:, 0, :n_experts], xg.shape[1], n_experts) for xg, _, cnt in routed]
    trs = [512 if h2d.shape[0] >= 4096 else 128 for h2d in h2ds]
    ygs = _experts(plans, [r[0] for r in routed], trs, w_gu, b_gu, w_d, b_d)
    return [_combine(yg, r[1], h2d).reshape(h.shape) for yg, r, h2d, h in zip(ygs, routed, h2ds, hs)]


def kernel(x_prompt, x_sample, mem_prompt, cache_k, cache_v, cache_mem_k, cache_mem_v, state_pool, page_table,
           g_mix, w_in, g_q, g_k, lam, g_sub, w_pool, pool_scale, w_out,
           g_xattn, g_mem, w_xq, w_xkv, g_xq, g_xk, w_xo,
           g_ffn, w_router, b_router, w_gate_up, b_gate_up, w_down, b_down):
    depth = w_in.shape[0]
    b_p, s_p, d = x_prompt.shape
    b_s, t_s, _ = x_sample.shape
    n_mem = mem_prompt.shape[1]
    n_experts = w_router.shape[2]
    assert LANES % n_experts == 0 and 3 * n_experts <= LANES
    assert ATT_WIDTH % MXU_DIM == 0 and (X_HEADS * X_HEAD_DIM) % MXU_DIM == 0

    slopes = jnp.asarray([2.0 ** (-8.0 * (h + 1) / N_HEADS) for h in range(N_HEADS)], F32)
    grp = jnp.arange(MXU_DIM) // HEAD_DIM
    gmat = (grp[:, None] == grp[None, :]).astype(BF16)
    row = lambda a: a.reshape(1, -1).astype(F32)

    hp, hs = x_prompt, x_sample
    outs = [[] for _ in range(8)]
    for l in range(depth):
        lam_init = 0.8 - 0.6 * math.exp(-0.3 * l)
        w_in_b = w_in[l].astype(BF16)
        w_out_b = w_out[l].astype(BF16)
        w_xq_b = w_xq[l].astype(BF16)
        w_xkv_b = w_xkv[l].astype(BF16)
        w_xo_b = w_xo[l].astype(BF16)
        gqk = jnp.concatenate([jnp.tile(g_q[l].reshape(-1), N_HEADS), jnp.tile(g_k[l].reshape(-1), N_HEADS)])[None, :]
        gxq = (jnp.tile(g_xq[l], X_HEADS) * (X_HEAD_DIM ** -0.5))[None, :]
        gxk = jnp.tile(g_xk[l], X_HEADS)[None, :]
        wbd = jax.scipy.linalg.block_diag(*[w_pool[l, g] for g in range(len(POOL_WINDOWS))]).astype(BF16)
        wr_pad_b = jnp.pad(w_router[l], ((0, 0), (0, LANES - n_experts))).astype(BF16)
        br_pad = jnp.pad(b_router[l].astype(F32), (0, LANES - n_experts), constant_values=NEG)[None, :]
        moe_w = (row(g_ffn[l]), wr_pad_b, br_pad, w_gate_up[l], b_gate_up[l], w_down[l], b_down[l])
        gsub = row(g_sub[l])

        q_b, k_b, v_t, k32, v32, up = _proj(hp.reshape(b_p * s_p, d), row(g_mix[l]), w_in_b, gqk, gmat)
        o_att = _attn(q_b, k_b, v_t, slopes, lam[l], g_sub[l].reshape(-1, 1).astype(F32), s_p, lam_init)
        up3 = up.reshape(b_p, s_p, -1)
        h1 = _outproj(hp, o_att.reshape(b_p, s_p, -1), up3, up3, wbd, row(pool_scale[l]), w_out_b, n_prev=0,
                      halo_from_up=True)
        mk, mv = _memkv(mem_prompt.reshape(b_p * n_mem, d), row(g_mem[l]), w_xkv_b, gxk, gmat)
        h2 = _xattn(h1, mk.reshape(b_p, n_mem, -1), mv.reshape(b_p, n_mem, -1), row(g_xattn[l]), w_xq_b, gxq, gmat,
                    w_xo_b)
        outs[0].append(k32.reshape(b_p, s_p, N_HEADS, V_DIM))
        outs[1].append(v32.reshape(b_p, s_p, N_HEADS, V_DIM))
        outs[2].append(up3[:, s_p - POOL_BUF:])
        outs[3].append(mk.reshape(b_p, n_mem, X_HEADS, X_HEAD_DIM))
        outs[4].append(mv.reshape(b_p, n_mem, X_HEADS, X_HEAD_DIM))

        q_s, _, _, k32_s, v32_s, up_s = _proj(hs.reshape(b_s * t_s, d), row(g_mix[l]), w_in_b, gqk, gmat)
        kn = k32_s.reshape(b_s, t_s, N_HEADS, V_DIM)
        vn = v32_s.reshape(b_s, t_s, N_HEADS, V_DIM)
        o_att_s = _sample_attn(q_s.reshape(b_s, t_s, -1), kn, vn, cache_k, cache_v, l, page_table, slopes, lam[l],
                               gsub, lam_init)
        ups = up_s.reshape(b_s, t_s, -1)
        halo_s = jnp.pad(state_pool[l], ((0, 0), (HALO - POOL_BUF, 0), (0, 0)))
        h1s = _outproj(hs, o_att_s, ups, halo_s, wbd, row(pool_scale[l]), w_out_b, n_prev=POOL_BUF, halo_from_up=False)
        h2s = _xattn(h1s, cache_mem_k[l].reshape(b_s, n_mem, -1), cache_mem_v[l].reshape(b_s, n_mem, -1),
                     row(g_xattn[l]), w_xq_b, gxq, gmat, w_xo_b)
        hp, hs = _moe([h2, h2s], *moe_w)
        outs[5].append(kn.reshape(b_s, t_s, N_HEADS, V_DIM))
        outs[6].append(vn.reshape(b_s, t_s, N_HEADS, V_DIM))
        outs[7].append(jnp.concatenate([state_pool[l], ups], axis=1)[:, t_s:])

    stacked = [jnp.stack(o) for o in outs]
    return (hp, hs, *stacked)
```

```python
import functools
import math

import jax
import jax.numpy as jnp
from jax import lax
from jax.experimental import pallas as pl
from jax.experimental.pallas import tpu as pltpu

F32 = jnp.float32
BF16 = jnp.bfloat16
I32 = jnp.int32

EPS = 1e-6
N_HEADS = 4
HEAD_DIM = 64
V_DIM = 2 * HEAD_DIM
ATT_WIDTH = N_HEADS * V_DIM
POOL_WINDOWS = (2, 4, 8, 16)
POOL_BUF = max(POOL_WINDOWS) - 1
HALO = POOL_BUF + 1
X_HEADS = 4
X_HEAD_DIM = 64
TOP_K = 4
SWIGLU_LIMIT = 7.0
SWIGLU_ALPHA = 1.702
NEG = -1e30

LANES = 128
SUBLANES = 8
MXU_DIM = 256
VMEM_LIMIT = 56 * 1024 * 1024


def _params(*sem):
    return pltpu.CompilerParams(dimension_semantics=sem, vmem_limit_bytes=VMEM_LIMIT)


def _tile(n, pref):
    return pref if n % pref == 0 else n


def _rms(x, g):
    return x * lax.rsqrt(jnp.mean(x * x, axis=-1, keepdims=True) + EPS) * g


def _dot(a, b):
    return jnp.dot(a, b, preferred_element_type=F32)


def _dot_nt(a, b):
    return lax.dot_general(a, b, (((1,), (1,)), ((), ())), preferred_element_type=F32)


def _dot_tn(a, b):
    return lax.dot_general(a, b, (((0,), (0,)), ((), ())), preferred_element_type=F32)


def _group_meansq(u, gmat):
    sq = u * u
    hi = sq.astype(BF16)
    lo = (sq - hi.astype(F32)).astype(BF16)
    outs = []
    for c in range(u.shape[1] // MXU_DIM):
        sl = slice(c * MXU_DIM, (c + 1) * MXU_DIM)
        outs.append(_dot(hi[:, sl], gmat) + _dot(lo[:, sl], gmat))
    return jnp.concatenate(outs, axis=1) * (1.0 / HEAD_DIM)


def _proj_kernel(x_ref, gmix_ref, w_ref, gqk_ref, gmat_ref, q_ref, kb_ref, vt_ref, k_ref, v_ref, up_ref):
    aw = ATT_WIDTH
    xn = _rms(x_ref[...], gmix_ref[...])
    u = _dot(xn.astype(BF16), w_ref[...])
    qk = u[:, :2 * aw]
    qkn = qk * lax.rsqrt(_group_meansq(qk, gmat_ref[...]) + EPS) * gqk_ref[...]
    kn = qkn[:, aw:]
    v = u[:, 2 * aw:3 * aw]
    for h in range(N_HEADS):
        hs = slice(h * V_DIM, (h + 1) * V_DIM)
        k_ref[:, h, :] = kn[:, hs]
        v_ref[:, h, :] = v[:, hs]
    up_ref[...] = u[:, 3 * aw:]
    q_ref[...] = (qkn[:, :aw] * (HEAD_DIM ** -0.5)).astype(BF16)
    kb_ref[...] = kn.astype(BF16)
    vt_ref[...] = v.T.astype(BF16)


def _proj(x2d, g_mix, w_in_b, gqk, gmat):
    n, d = x2d.shape
    wtot = w_in_b.shape[1]
    aw = ATT_WIDTH
    tm = _tile(n, 512)
    full = lambda shape: pl.BlockSpec(shape, lambda i: (0,) * len(shape))
    rows = lambda w: pl.BlockSpec((tm, w), lambda i: (i, 0))
    heads = pl.BlockSpec((tm, N_HEADS, V_DIM), lambda i: (i, 0, 0))
    return pl.pallas_call(
        _proj_kernel,
        grid=(n // tm,),
        in_specs=[rows(d), full((1, d)), full((d, wtot)), full((1, 2 * aw)), full((MXU_DIM, MXU_DIM))],
        out_specs=[rows(aw), rows(aw), pl.BlockSpec((aw, tm), lambda i: (0, i)), heads, heads, rows(wtot - 3 * aw)],
        out_shape=[jax.ShapeDtypeStruct((n, aw), BF16),
                   jax.ShapeDtypeStruct((n, aw), BF16),
                   jax.ShapeDtypeStruct((aw, n), BF16),
                   jax.ShapeDtypeStruct((n, N_HEADS, V_DIM), F32),
                   jax.ShapeDtypeStruct((n, N_HEADS, V_DIM), F32),
                   jax.ShapeDtypeStruct((n, wtot - 3 * aw), F32)],
        compiler_params=_params("parallel"),
        name="in_proj",
    )(x2d, g_mix, w_in_b, gqk, gmat)


def _lambda_value(lam_ref, lam_init):
    lf = lam_ref[...]
    a = jnp.sum(lf[0:1] * lf[1:2], axis=-1, keepdims=True)
    b = jnp.sum(lf[2:3] * lf[3:4], axis=-1, keepdims=True)
    return jnp.exp(a) - jnp.exp(b) + lam_init


def _sub_norm(od, gsub, lam_init):
    return _rms(od, gsub) * (1.0 - lam_init)


def _softmax_step(s, vb, m, l, acc):
    m_new = jnp.maximum(m, jnp.max(s, axis=-1, keepdims=True))
    alpha = jnp.exp(m - m_new)
    p = jnp.exp(s - m_new)
    l = alpha * l + jnp.sum(p, axis=-1, keepdims=True)
    acc = alpha * acc + _dot(p.astype(BF16), vb)
    return m_new, l, acc


POS_SPLIT = 64
ONES_ROWS = 2 * SUBLANES


def _attn_kernel(slopes_ref, lam_ref, gsub_ref, q_ref, k_ref, vt_ref, pf_ref, o_ref, acc_ref, s0_ref, s1_ref,
                 *, t, lam_init):
    h = pl.program_id(1)
    i = pl.program_id(2)
    slope = slopes_ref[h]
    q = q_ref[...]
    lane = lax.broadcasted_iota(I32, q.shape, 1)
    zero = jnp.zeros_like(q)
    feat = jnp.where(lane == 0, slope * POS_SPLIT, jnp.where(lane == 1, slope, 0.0)).astype(BF16)
    qa = (jnp.concatenate([jnp.where(lane < HEAD_DIM, q, zero), feat], axis=1),
          jnp.concatenate([jnp.where(lane >= HEAD_DIM, q, zero), feat], axis=1))
    acc_ref[...] = jnp.zeros(acc_ref.shape, F32)
    s_bufs = (s0_ref, s1_ref)

    def scores(j, buf):
        k0 = pl.multiple_of(j * t, t)
        kaug = jnp.concatenate([k_ref[pl.ds(k0, t), :], pf_ref[pl.ds(k0, t), :]], axis=1)
        for c in range(2):
            s_bufs[buf][c] = _dot_nt(kaug, qa[c])

    def absorb(j, buf, ms, masked):
        k0 = pl.multiple_of(j * t, t)
        vt = jnp.concatenate([vt_ref[:, pl.ds(k0, t)], jnp.ones((ONES_ROWS, t), BF16)], axis=0)
        out = []
        for c in range(2):
            s = s_bufs[buf][c]
            if masked:
                kr = lax.broadcasted_iota(I32, s.shape, 0)
                qc = lax.broadcasted_iota(I32, s.shape, 1)
                s = jnp.where(kr <= qc, s, NEG)
            m_new = jnp.maximum(ms[c], jnp.max(s, axis=0, keepdims=True))
            alpha = jnp.exp(ms[c] - m_new)
            acc_ref[c] = alpha * acc_ref[c] + _dot(vt, jnp.exp(s - m_new).astype(BF16))
            out.append(m_new)
        return tuple(out)

    def pair(jj, ms):
        j = 2 * jj
        scores(j + 1, 1)
        ms = absorb(j, 0, ms, False)
        scores(j + 2, 0)
        return absorb(j + 1, 1, ms, False)

    scores(0, 0)
    ms = lax.fori_loop(0, i // 2, pair, (jnp.full((1, t), NEG, F32),) * 2)

    @pl.when(i % 2 == 0)
    def _():
        absorb(i, 0, ms, True)

    @pl.when(i % 2 == 1)
    def _():
        scores(i, 1)
        absorb(i, 1, absorb(i - 1, 0, ms, False), True)

    o1 = acc_ref[0, :V_DIM, :] / acc_ref[0, V_DIM:V_DIM + 1, :]
    o2 = acc_ref[1, :V_DIM, :] / acc_ref[1, V_DIM:V_DIM + 1, :]
    od = o1 - _lambda_value(lam_ref, lam_init) * o2
    y = od * lax.rsqrt(jnp.mean(od * od, axis=0, keepdims=True) + EPS) * gsub_ref[...] * (1.0 - lam_init)
    o_ref[...] = y.T.astype(o_ref.dtype)


def _attn(q, kb, vt, slopes, lam, gsub_col, seq, lam_init):
    n = q.shape[0]
    b = n // seq
    t = _tile(seq, 512)
    nq = seq // t
    assert seq <= POS_SPLIT * 256, "key positions must split into two bf16-exact factors"
    pos = jnp.arange(seq)
    pf = jnp.zeros((seq, V_DIM), F32).at[:, 0].set(pos // POS_SPLIT).at[:, 1].set(pos % POS_SPLIT).astype(BF16)
    kern = functools.partial(_attn_kernel, t=t, lam_init=lam_init)
    return pl.pallas_call(
        kern,
        grid=(b, N_HEADS, nq),
        in_specs=[pl.BlockSpec(memory_space=pltpu.SMEM),
                  pl.BlockSpec(lam.shape, lambda bi, h, i: (0, 0)),
                  pl.BlockSpec((V_DIM, 1), lambda bi, h, i: (0, 0)),
                  pl.BlockSpec((t, V_DIM), lambda bi, h, i: (bi * nq + i, h)),
                  pl.BlockSpec((seq, V_DIM), lambda bi, h, i: (bi, h)),
                  pl.BlockSpec((V_DIM, seq), lambda bi, h, i: (h, bi)),
                  pl.BlockSpec((seq, V_DIM), lambda bi, h, i: (0, 0))],
        out_specs=pl.BlockSpec((t, V_DIM), lambda bi, h, i: (bi * nq + i, h)),
        out_shape=jax.ShapeDtypeStruct((n, ATT_WIDTH), BF16),
        scratch_shapes=[pltpu.VMEM((2, V_DIM + ONES_ROWS, t), F32),
                        pltpu.VMEM((2, t, t), F32),
                        pltpu.VMEM((2, t, t), F32)],
        compiler_params=_params("parallel", "parallel", "arbitrary"),
        name="prompt_attn",
    )(slopes, lam, gsub_col, q, kb, vt, pf)


def _page_copies(pt_ref, ck_hbm, cv_hbm, kbuf, vbuf, sem, step, slot, *, layer, n_chunks, pages_per_chunk, page):
    b = step // n_chunks
    c = step % n_chunks
    n_pages = n_chunks * pages_per_chunk
    copies = []
    for p in range(pages_per_chunk):
        phys = pt_ref[b * n_pages + c * pages_per_chunk + p]
        dst = pl.ds(p * page, page)
        copies.append(pltpu.make_async_copy(ck_hbm.at[layer, phys], kbuf.at[slot, dst], sem.at[0, slot]))
        copies.append(pltpu.make_async_copy(cv_hbm.at[layer, phys], vbuf.at[slot, dst], sem.at[1, slot]))
    return copies


def _sample_attn_kernel(pt_ref, slopes_ref, lam_ref, gsub_ref, q_ref, kn_ref, vn_ref, ck_hbm, cv_hbm, o_ref,
                        kbuf, vbuf, sem, m_ref, l_ref, acc_ref, bias_ref, *, layer, n_chunks, pages_per_chunk, page,
                        t_new, lam_init):
    b = pl.program_id(0)
    c = pl.program_id(1)
    step = b * n_chunks + c
    n_steps = pl.num_programs(0) * n_chunks
    slot = step % 2
    copies = functools.partial(_page_copies, pt_ref, ck_hbm, cv_hbm, kbuf, vbuf, sem, layer=layer,
                               n_chunks=n_chunks, pages_per_chunk=pages_per_chunk, page=page)

    def start_all(cps):
        for n, cp in enumerate(cps):
            cp.start(priority=n % 2)

    @pl.when(step == 0)
    def _():
        start_all(copies(step, slot))

    @pl.when(step + 1 < n_steps)
    def _():
        start_all(copies(step + 1, 1 - slot))

    @pl.when(c == 0)
    def _():
        m_ref[...] = jnp.full(m_ref.shape, NEG, F32)
        l_ref[...] = jnp.zeros(l_ref.shape, F32)
        acc_ref[...] = jnp.zeros(acc_ref.shape, F32)

    qh = q_ref[0]
    n_rows = qh.shape[0]
    ck = pages_per_chunk * page
    keys = ck // N_HEADS
    past = n_chunks * keys
    rows = lax.broadcasted_iota(I32, (n_rows, 1), 0)
    tok = rows % t_new
    head = rows // (2 * t_new)
    slope = jnp.zeros((n_rows, 1), F32)
    for hh in range(N_HEADS):
        slope = jnp.where(head == hh, slopes_ref[hh], slope)

    @pl.when(step == 0)
    def _():
        cols = lax.broadcasted_iota(I32, (n_rows, ck), 1)
        bias_ref[...] = jnp.where(cols % N_HEADS == head, slope * (cols // N_HEADS).astype(F32), NEG)

    for cp in copies(step, slot):
        cp.wait()

    kb = kbuf[slot].astype(BF16)
    vb = vbuf[slot].astype(BF16)
    s = _dot_nt(qh, kb) + bias_ref[...] + slope * (c * keys - past - tok).astype(F32)
    m, l, acc = _softmax_step(s, vb, m_ref[...], l_ref[...], acc_ref[...])
    m_ref[...] = m
    l_ref[...] = l
    acc_ref[...] = acc

    @pl.when(c == n_chunks - 1)
    def _():
        knb = kn_ref[0].astype(BF16)
        vnb = vn_ref[0].astype(BF16)
        ncol = lax.broadcasted_iota(I32, (n_rows, knb.shape[0]), 1)
        nkey = ncol // N_HEADS
        nd = tok - nkey
        sn = _dot_nt(qh, knb) - slope * nd.astype(F32)
        sn = jnp.where((nd >= 0) & (nkey < t_new) & (ncol % N_HEADS == head), sn, NEG)
        _, l2, acc2 = _softmax_step(sn, vnb, m, l, acc)
        o = acc2 / l2
        od = o - _lambda_value(lam_ref, lam_init) * pltpu.roll(o, n_rows - t_new, axis=0)
        o_ref[0] = _sub_norm(od, gsub_ref[...], lam_init).astype(o_ref.dtype)


def _sample_attn(q, kn, vn, cache_k, cache_v, layer, page_table, slopes, lam, gsub, lam_init):
    b, t_new, width = q.shape
    depth, n_phys, page = cache_k.shape[:3]
    n_pages = page_table.shape[1]
    ppc = 16 if n_pages % 16 == 0 else n_pages
    n_chunks = n_pages // ppc
    n_rows = N_HEADS * 2 * t_new
    pairs = page * N_HEADS
    qh = q.reshape(b, t_new, N_HEADS, V_DIM).transpose(0, 2, 1, 3)
    map_mask = (jnp.arange(V_DIM)[None, :] // HEAD_DIM == jnp.arange(2)[:, None]).astype(q.dtype)
    qh = (qh[:, :, None] * map_mask[None, None, :, None, :]).reshape(b, n_rows, V_DIM)
    new_rows = lambda a: jnp.pad(a, ((0, 0), (0, SUBLANES - t_new), (0, 0), (0, 0))).reshape(b, -1, V_DIM)
    kern = functools.partial(_sample_attn_kernel, layer=layer, n_chunks=n_chunks, pages_per_chunk=ppc, page=pairs,
                             t_new=t_new, lam_init=lam_init)
    grid_spec = pltpu.PrefetchScalarGridSpec(
        num_scalar_prefetch=1,
        grid=(b, n_chunks),
        in_specs=[pl.BlockSpec(memory_space=pltpu.SMEM),
                  pl.BlockSpec(lam.shape, lambda bi, c, pt: (0, 0)),
                  pl.BlockSpec((1, V_DIM), lambda bi, c, pt: (0, 0)),
                  pl.BlockSpec((1, n_rows, V_DIM), lambda bi, c, pt: (bi, 0, 0)),
                  pl.BlockSpec((1, SUBLANES * N_HEADS, V_DIM), lambda bi, c, pt: (bi, 0, 0)),
                  pl.BlockSpec((1, SUBLANES * N_HEADS, V_DIM), lambda bi, c, pt: (bi, 0, 0)),
                  pl.BlockSpec(memory_space=pl.ANY),
                  pl.BlockSpec(memory_space=pl.ANY)],
        out_specs=pl.BlockSpec((1, n_rows, V_DIM), lambda bi, c, pt: (bi, 0, 0)),
        scratch_shapes=[pltpu.VMEM((2, ppc * pairs, V_DIM), F32),
                        pltpu.VMEM((2, ppc * pairs, V_DIM), F32),
                        pltpu.SemaphoreType.DMA((2, 2)),
                        pltpu.VMEM((n_rows, 1), F32),
                        pltpu.VMEM((n_rows, 1), F32),
                        pltpu.VMEM((n_rows, V_DIM), F32),
                        pltpu.VMEM((n_rows, ppc * pairs), F32)])
    out = pl.pallas_call(
        kern,
        grid_spec=grid_spec,
        out_shape=jax.ShapeDtypeStruct((b, n_rows, V_DIM), BF16),
        compiler_params=_params("arbitrary", "arbitrary"),
        name="sample_attn",
    )(page_table.reshape(-1), slopes, lam, gsub, qh, new_rows(kn), new_rows(vn),
      cache_k.reshape(depth, n_phys, pairs, V_DIM), cache_v.reshape(depth, n_phys, pairs, V_DIM))
    out = out.reshape(b, N_HEADS, 2, t_new, V_DIM)[:, :, 0]
    return out.transpose(0, 2, 1, 3).reshape(b, t_new, width)


def _outproj_kernel(x_ref, oatt_ref, up_ref, halo_ref, wbd_ref, pscale_ref, wout_ref, o_ref, ext_ref, *,
                    tm, n_prev, zero_first_halo):
    i = pl.program_id(1)
    halo = halo_ref[0]
    if zero_first_halo:
        halo = jnp.where(i == 0, 0.0, halo)
    cur = up_ref[0]
    ext_ref[0:HALO, :] = halo
    ext_ref[HALO:HALO + tm, :] = cur
    gw = cur.shape[1] // len(POOL_WINDOWS)
    pos1 = lax.broadcasted_iota(I32, (tm, gw), 0) + (i * tm + n_prev + 1)
    ds = []
    for g, w in enumerate(POOL_WINDOWS):
        cs = slice(g * gw, (g + 1) * gw)
        acc = cur[:, cs]
        for sft in range(1, w):
            acc = acc + ext_ref[HALO - sft:HALO - sft + tm, cs]
        cnt = jnp.minimum(pos1, w).astype(F32)
        ds.append(acc / cnt - cur[:, cs])
    d = jnp.concatenate(ds, axis=1)
    o_pool = _dot(d.astype(BF16), wbd_ref[...]) * pscale_ref[...]
    aw = oatt_ref.shape[2]
    o_ref[0] = (x_ref[0] + _dot(oatt_ref[0], wout_ref[0:aw, :])
                + _dot(o_pool.astype(BF16), wout_ref[aw:, :]))


def _outproj(x, o_att, up, halo_arr, wbd, pool_scale, w_out_b, *, n_prev, halo_from_up):
    b, t, d = x.shape
    aw = o_att.shape[2]
    pw = up.shape[2]
    tm = _tile(t, 512)
    if halo_from_up:
        halo_map = lambda bi, i: (bi, jnp.maximum(i * (tm // HALO) - 1, 0), 0)
    else:
        halo_map = lambda bi, i: (bi, 0, 0)
    kern = functools.partial(_outproj_kernel, tm=tm, n_prev=n_prev, zero_first_halo=halo_from_up)
    full = lambda shape: pl.BlockSpec(shape, lambda bi, i: (0,) * len(shape))
    return pl.pallas_call(
        kern,
        grid=(b, t // tm),
        in_specs=[pl.BlockSpec((1, tm, d), lambda bi, i: (bi, i, 0)),
                  pl.BlockSpec((1, tm, aw), lambda bi, i: (bi, i, 0)),
                  pl.BlockSpec((1, tm, pw), lambda bi, i: (bi, i, 0)),
                  pl.BlockSpec((1, HALO, pw), halo_map),
                  full((pw, pw)), full((1, pw)), full((aw + pw, d))],
        out_specs=pl.BlockSpec((1, tm, d), lambda bi, i: (bi, i, 0)),
        out_shape=jax.ShapeDtypeStruct((b, t, d), F32),
        scratch_shapes=[pltpu.VMEM((HALO + tm, pw), F32)],
        compiler_params=_params("parallel", "arbitrary"),
        name="pool_out_proj",
    )(x, o_att, up, halo_arr, wbd, pool_scale, w_out_b)


def _memkv_kernel(mem_ref, gmem_ref, w_ref, gxk_ref, gmat_ref, mk_ref, mv_ref):
    xw = mk_ref.shape[1]
    kv = _dot(_rms(mem_ref[...], gmem_ref[...]).astype(BF16), w_ref[...])
    kx = kv[:, :xw]
    mk_ref[...] = kx * lax.rsqrt(_group_meansq(kx, gmat_ref[...]) + EPS) * gxk_ref[...]
    mv_ref[...] = kv[:, xw:]


def _memkv(mem2d, g_mem, w_xkv_b, gxk, gmat):
    n, d = mem2d.shape
    xw = w_xkv_b.shape[1] // 2
    tm = _tile(n, 512)
    full = lambda shape: pl.BlockSpec(shape, lambda i: (0,) * len(shape))
    return pl.pallas_call(
        _memkv_kernel,
        grid=(n // tm,),
        in_specs=[pl.BlockSpec((tm, d), lambda i: (i, 0)), full((1, d)), full((d, 2 * xw)),
                  full((1, xw)), full((MXU_DIM, MXU_DIM))],
        out_specs=[pl.BlockSpec((tm, xw), lambda i: (i, 0))] * 2,
        out_shape=[jax.ShapeDtypeStruct((n, xw), F32)] * 2,
        compiler_params=_params("parallel"),
        name="mem_kv",
    )(mem2d, g_mem, w_xkv_b, gxk, gmat)


def _xattn_kernel(h_ref, mk_ref, mv_ref, gx_ref, wq_ref, gxq_ref, gmat_ref, wo_ref, o_ref):
    h1 = h_ref[0]
    qx = _dot(_rms(h1, gx_ref[...]).astype(BF16), wq_ref[...])
    qn = (qx * lax.rsqrt(_group_meansq(qx, gmat_ref[...]) + EPS) * gxq_ref[...]).astype(BF16)
    mk = mk_ref[0].astype(BF16)
    mv = mv_ref[0].astype(BF16)
    head_of_lane = lax.broadcasted_iota(I32, qn.shape, 1) // X_HEAD_DIM
    o = jnp.zeros(qn.shape, F32)
    for hh in range(X_HEADS):
        sel = head_of_lane == hh
        s = _dot_nt(jnp.where(sel, qn, jnp.zeros_like(qn)), mk)
        p = jnp.exp(s - jnp.max(s, axis=-1, keepdims=True))
        p = p / jnp.sum(p, axis=-1, keepdims=True)
        o = o + jnp.where(sel, _dot(p.astype(BF16), mv), 0.0)
    o_ref[0] = h1 + _dot(o.astype(BF16), wo_ref[...])


def _xattn(h1, mk, mv, g_xattn, w_xq_b, gxq, gmat, w_xo_b):
    b, t, d = h1.shape
    nm, xw = mk.shape[1], mk.shape[2]
    tq = _tile(t, 512)
    full = lambda shape: pl.BlockSpec(shape, lambda bi, i: (0,) * len(shape))
    return pl.pallas_call(
        _xattn_kernel,
        grid=(b, t // tq),
        in_specs=[pl.BlockSpec((1, tq, d), lambda bi, i: (bi, i, 0)),
                  pl.BlockSpec((1, nm, xw), lambda bi, i: (bi, 0, 0)),
                  pl.BlockSpec((1, nm, xw), lambda bi, i: (bi, 0, 0)),
                  full((1, d)), full((d, xw)), full((1, xw)), full((MXU_DIM, MXU_DIM)), full((xw, d))],
        out_specs=pl.BlockSpec((1, tq, d), lambda bi, i: (bi, i, 0)),
        out_shape=jax.ShapeDtypeStruct((b, t, d), F32),
        compiler_params=_params("parallel", "parallel"),
        name="mem_xattn",
    )(h1, mk, mv, g_xattn, w_xq_b, gxq, gmat, w_xo_b)


def _gather_rows(t, n_experts):
    return -(-(TOP_K * t + n_experts * (SUBLANES - 1)) // LANES) * LANES


def _slot_onehot(pos, n_rows):
    t = pos.shape[0]
    r = lax.broadcasted_iota(I32, (t, n_rows), 1)
    onehot = jnp.zeros((t, n_rows), F32)
    for k in range(TOP_K):
        onehot = jnp.where(r == pos[:, k:k + 1], 1.0, onehot)
    return onehot.astype(BF16)


def _route_kernel(h_ref, gffn_ref, wr_ref, br_ref, xg_ref, pos_ref, cnt_ref, *, n_experts):
    t = h_ref.shape[0]
    n_rows = xg_ref.shape[1]
    mb = _rms(h_ref[...], gffn_ref[...]).astype(BF16)
    logits = _dot(mb, wr_ref[...]) + br_ref[...]
    lane = lax.broadcasted_iota(I32, logits.shape, 1).astype(F32)
    work = logits
    sels, vals = [], []
    for _ in range(TOP_K):
        mx = jnp.max(work, axis=-1, keepdims=True)
        idx = jnp.min(jnp.where(work == mx, lane, float(LANES)), axis=-1, keepdims=True)
        sel = lane == idx
        sels.append(sel)
        vals.append(mx)
        work = jnp.where(sel, 2.0 * NEG, work)
    es = [jnp.exp(v - vals[0]) for v in vals]
    denom = es[0]
    for e in es[1:]:
        denom = denom + e
    hot = jnp.zeros(logits.shape, F32)
    gate_m = jnp.zeros(logits.shape, F32)
    for sel, e in zip(sels, es):
        hot = jnp.where(sel, 1.0, hot)
        gate_m = jnp.where(sel, e / denom, gate_m)
    ri = lax.broadcasted_iota(I32, (t, t), 0)
    ci = lax.broadcasted_iota(I32, (t, t), 1)
    rank = _dot(jnp.where(ci < ri, 1.0, 0.0).astype(BF16), hot.astype(BF16))
    count = jnp.sum(hot, axis=0, keepdims=True).astype(I32)
    c8 = (((count + (SUBLANES - 1)) >> 3) << 3).astype(F32)
    ei = lax.broadcasted_iota(I32, (LANES, LANES), 0)
    ej = lax.broadcasted_iota(I32, (LANES, LANES), 1)
    off = _dot(jnp.broadcast_to(c8, (SUBLANES, LANES)).astype(BF16), jnp.where(ei < ej, 1.0, 0.0).astype(BF16))[0:1]
    slot = off + rank
    lane_i = lax.broadcasted_iota(I32, logits.shape, 1)
    pos = jnp.full(logits.shape, -1, I32)
    for k, sel in enumerate(sels):
        pk = jnp.sum(jnp.where(sel, slot, 0.0), axis=-1, keepdims=True).astype(I32)
        pos = jnp.where(lane_i == k, pk, pos)
    pos_ref[...] = pos
    cnt_ref[0] = jnp.broadcast_to(count, (SUBLANES, LANES))
    g1 = gate_m.astype(BF16)
    r1 = gate_m - g1.astype(F32)
    g2 = r1.astype(BF16)
    g3 = (r1 - g2.astype(F32)).astype(BF16)
    gx = (g1.astype(F32) + pltpu.roll(g2.astype(F32), n_experts, axis=1)
          + pltpu.roll(g3.astype(F32), 2 * n_experts, axis=1)).astype(BF16)
    xg_ref[0] = _dot_tn(_slot_onehot(pos, n_rows), jnp.concatenate([mb, gx], axis=1))


def _route(h2d, g_ffn, wr_pad_b, br_pad, n_experts):
    n, d = h2d.shape
    t = _tile(n, 256)
    n_sub = n // t
    n_rows = _gather_rows(t, n_experts)
    kern = functools.partial(_route_kernel, n_experts=n_experts)
    full = lambda shape: pl.BlockSpec(shape, lambda i: (0,) * len(shape))
    return pl.pallas_call(
        kern,
        grid=(n_sub,),
        in_specs=[pl.BlockSpec((t, d), lambda i: (i, 0)), full((1, d)), full((d, LANES)), full((1, LANES))],
        out_specs=[pl.BlockSpec((1, n_rows, d + LANES), lambda i: (i, 0, 0)),
                   pl.BlockSpec((t, LANES), lambda i: (i, 0)),
                   pl.BlockSpec((1, SUBLANES, LANES), lambda i: (i, 0, 0))],
        out_shape=[jax.ShapeDtypeStruct((n_sub, n_rows, d + LANES), F32),
                   jax.ShapeDtypeStruct((n, LANES), I32),
                   jax.ShapeDtypeStruct((n_sub, SUBLANES, LANES), I32)],
        compiler_params=_params("parallel"),
        name="moe_route_gather",
    )(h2d, g_ffn, wr_pad_b, br_pad)


def _unit_plan(cnt, n_rows, n_experts):
    n_sub = cnt.shape[0]
    upt = n_rows // SUBLANES
    u = (cnt + (SUBLANES - 1)) // SUBLANES
    uoff = jnp.cumsum(u, axis=1) - u
    used = jnp.sum(u, axis=1, keepdims=True)
    u_all = jnp.concatenate([u, upt - used], axis=1)
    uoff_all = jnp.concatenate([uoff, used], axis=1)
    flat_cnt = u_all.T.reshape(-1)
    cum = jnp.cumsum(flat_cnt)
    start = cum - flat_cnt
    k = jnp.arange(upt, dtype=I32)
    grp = jnp.sum(uoff_all[:, None, 1:] <= k[None, :, None], axis=-1)
    shift = start.reshape(n_experts + 1, n_sub).T - uoff_all
    hit = grp[:, :, None] == jnp.arange(n_experts + 1, dtype=I32)[None, None, :]
    dest = k[None, :] + jnp.sum(jnp.where(hit, shift[:, None, :], 0), axis=-1)
    n_units = n_sub * upt
    unit_src = jnp.zeros((n_units,), I32).at[dest.reshape(-1)].set(jnp.arange(n_units, dtype=I32))
    ustart = jnp.concatenate([start[::n_sub], cum[-1:]]).astype(I32)
    return ustart, unit_src


def _expert_stream(k, ustart_ref, usrc_ref, xg_hbm, yg_hbm, xbuf, ybuf, in_sem, out_sem, bgu_ref, bd_ref, wgu_b, wd_b,
                   *, n_experts, upb, d_model):
    e = pl.program_id(0)
    u0 = ustart_ref[e]
    u1 = ustart_ref[e + 1]
    dff = wd_b.shape[0]

    def in_copy(j, slot, i):
        return pltpu.make_async_copy(xg_hbm.at[usrc_ref[j]], xbuf.at[slot, pl.ds(i * SUBLANES, SUBLANES)],
                                     in_sem.at[k, slot])

    def out_copy(j, i):
        return pltpu.make_async_copy(ybuf.at[pl.ds(i * SUBLANES, SUBLANES)], yg_hbm.at[usrc_ref[j]], out_sem.at[k])

    def n_valid(blk):
        return jnp.minimum(upb, u1 - (u0 + blk * upb))

    def for_units(blk, fn):
        base = u0 + blk * upb
        nv = n_valid(blk)

        @pl.when(nv == upb)
        def _():
            for i in range(upb):
                fn(base + i, i)

        @pl.when(nv < upb)
        def _():
            def body(i, c):
                fn(base + i, i)
                return c
            lax.fori_loop(0, nv, body, 0)

    @pl.when(e == 0)
    def _():
        xbuf[...] = jnp.zeros(xbuf.shape, F32)

    @pl.when(e < n_experts)
    def _():
        n_blk = (u1 - u0 + upb - 1) // upb

        @pl.when(n_blk > 0)
        def _():
            for_units(0, lambda j, i: in_copy(j, 0, i).start())

        def block(blk, c):
            slot = blk % 2

            @pl.when(blk + 1 < n_blk)
            def _():
                for_units(blk + 1, lambda j, i: in_copy(j, 1 - slot, i).start())

            for_units(blk, lambda j, i: in_copy(j, slot, i).wait())
            x = xbuf[slot]
            gcols = x[:, d_model:]
            lane = lax.broadcasted_iota(I32, gcols.shape, 1)
            own = (lane % n_experts == e) & (lane < 3 * n_experts)
            gate_w = jnp.sum(jnp.where(own, gcols, 0.0), axis=-1, keepdims=True)
            xb = x[:, :d_model].astype(BF16)
            hidden = []
            for c0 in range(0, dff, MXU_DIM):
                gs = slice(c0, c0 + MXU_DIM)
                us = slice(dff + c0, dff + c0 + MXU_DIM)
                gate = jnp.minimum(_dot(xb, wgu_b[:, gs]) + bgu_ref[0, :, gs], SWIGLU_LIMIT)
                up = jnp.clip(_dot(xb, wgu_b[:, us]) + bgu_ref[0, :, us], -SWIGLU_LIMIT, SWIGLU_LIMIT)
                glu = gate * (1.0 / (1.0 + jnp.exp(-SWIGLU_ALPHA * gate)))
                hidden.append(((up + 1.0) * glu).astype(BF16))
            y = (_dot(jnp.concatenate(hidden, axis=1), wd_b[...]) + bd_ref[0]) * gate_w

            @pl.when(blk > 0)
            def _():
                for_units(blk - 1, lambda j, i: out_copy(j, i).wait())

            ybuf[...] = y
            for_units(blk, lambda j, i: out_copy(j, i).start())
            return c

        lax.fori_loop(0, n_blk, block, 0)

        @pl.when(n_blk > 0)
        def _():
            for_units(n_blk - 1, lambda j, i: out_copy(j, i).wait())

    @pl.when(e == n_experts)
    def _():
        ybuf[...] = jnp.zeros(ybuf.shape, F32)

        def start(j, c):
            out_copy(j, 0).start()
            return c

        def wait(j, c):
            out_copy(j, 0).wait()
            return c
        lax.fori_loop(u0, u1, start, 0)
        lax.fori_loop(u0, u1, wait, 0)


def _expert_kernel(*refs, n_experts, upbs, d_model):
    n = len(upbs)
    plans = refs[:2 * n]
    xg_hbms = refs[2 * n:3 * n]
    wgu_ref, bgu_ref, wd_ref, bd_ref = refs[3 * n:3 * n + 4]
    yg_hbms = refs[3 * n + 4:4 * n + 4]
    xbufs = refs[4 * n + 4:5 * n + 4]
    ybufs = refs[5 * n + 4:6 * n + 4]
    wgu_b, wd_b, in_sem, out_sem = refs[6 * n + 4:]

    @pl.when(pl.program_id(0) < n_experts)
    def _():
        rc = 128
        for r in range(0, wgu_b.shape[0], rc):
            wgu_b[r:r + rc, :] = wgu_ref[0, r:r + rc, :].astype(BF16)
        for r in range(0, wd_b.shape[0], rc):
            wd_b[r:r + rc, :] = wd_ref[0, r:r + rc, :].astype(BF16)

    for k in range(n):
        _expert_stream(k, plans[2 * k], plans[2 * k + 1], xg_hbms[k], yg_hbms[k], xbufs[k], ybufs[k], in_sem, out_sem,
                       bgu_ref, bd_ref, wgu_b, wd_b, n_experts=n_experts, upb=upbs[k], d_model=d_model)


def _experts(plans, xgs, trs, w_gu, b_gu, w_d, b_d):
    n = len(xgs)
    n_experts, d_model, dff2 = w_gu.shape
    dff = w_d.shape[1]
    dx = xgs[0].shape[2]
    units = [x.shape[0] * x.shape[1] // SUBLANES for x in xgs]
    kern = functools.partial(_expert_kernel, n_experts=n_experts, upbs=tuple(tr // SUBLANES for tr in trs),
                             d_model=d_model)
    last = n_experts - 1
    wmap = lambda e, *_: (jnp.minimum(e, last), 0, 0)
    grid_spec = pltpu.PrefetchScalarGridSpec(
        num_scalar_prefetch=2 * n,
        grid=(n_experts + 1,),
        in_specs=[pl.BlockSpec(memory_space=pl.ANY)] * n + [
            pl.BlockSpec((1, d_model, dff2), wmap), pl.BlockSpec((1, 1, dff2), wmap),
            pl.BlockSpec((1, dff, d_model), wmap), pl.BlockSpec((1, 1, d_model), wmap)],
        out_specs=[pl.BlockSpec(memory_space=pl.ANY)] * n,
        scratch_shapes=[pltpu.VMEM((2, tr, dx), F32) for tr in trs] + [pltpu.VMEM((tr, d_model), F32) for tr in trs] + [
            pltpu.VMEM((d_model, dff2), BF16),
            pltpu.VMEM((dff, d_model), BF16),
            pltpu.SemaphoreType.DMA((n, 2)),
            pltpu.SemaphoreType.DMA((n,))])
    ygs = pl.pallas_call(
        kern,
        grid_spec=grid_spec,
        out_shape=[jax.ShapeDtypeStruct((u, SUBLANES, d_model), F32) for u in units],
        compiler_params=_params("arbitrary"),
        name="moe_experts",
    )(*[a for plan in plans for a in plan], *[x.reshape(u, SUBLANES, dx) for x, u in zip(xgs, units)],
      w_gu, b_gu.reshape(n_experts, 1, dff2), w_d, b_d.reshape(n_experts, 1, d_model))
    return [y.reshape(x.shape[0], x.shape[1], d_model) for y, x in zip(ygs, xgs)]


def _combine_kernel(yg_ref, pos_ref, h_ref, o_ref):
    onehot = _slot_onehot(pos_ref[...], yg_ref.shape[1])
    o_ref[...] = h_ref[...] + _dot(onehot, yg_ref[0].astype(BF16))


def _combine(yg, pos, h2d):
    n_sub, n_rows, d = yg.shape
    n = h2d.shape[0]
    t = n // n_sub
    return pl.pallas_call(
        _combine_kernel,
        grid=(n_sub,),
        in_specs=[pl.BlockSpec((1, n_rows, d), lambda i: (i, 0, 0)),
                  pl.BlockSpec((t, LANES), lambda i: (i, 0)),
                  pl.BlockSpec((t, d), lambda i: (i, 0))],
        out_specs=pl.BlockSpec((t, d), lambda i: (i, 0)),
        out_shape=jax.ShapeDtypeStruct((n, d), F32),
        compiler_params=_params("parallel"),
        name="moe_combine",
    )(yg, pos, h2d)


def _moe(hs, g_ffn, wr_pad_b, br_pad, w_gu, b_gu, w_d, b_d):
    n_experts = w_gu.shape[0]
    h2ds = [h.reshape(-1, h.shape[-1]) for h in hs]
    routed = [_route(h2d, g_ffn, wr_pad_b, br_pad, n_experts) for h2d in h2ds]
    plans = [_unit_plan(cnt[:, 0, :n_experts], xg.shape[1], n_experts) for xg, _, cnt in routed]
    trs = [512 if h2d.shape[0] >= 4096 else 128 for h2d in h2ds]
    ygs = _experts(plans, [r[0] for r in routed], trs, w_gu, b_gu, w_d, b_d)
    return [_combine(yg, r[1], h2d).reshape(h.shape) for yg, r, h2d, h in zip(ygs, routed, h2ds, hs)]


def kernel(x_prompt, x_sample, mem_prompt, cache_k, cache_v, cache_mem_k, cache_mem_v, state_pool, page_table,
           g_mix, w_in, g_q, g_k, lam, g_sub, w_pool, pool_scale, w_out,
           g_xattn, g_mem, w_xq, w_xkv, g_xq, g_xk, w_xo,
           g_ffn, w_router, b_router, w_gate_up, b_gate_up, w_down, b_down):
    depth = w_in.shape[0]
    b_p, s_p, d = x_prompt.shape
    b_s, t_s, _ = x_sample.shape
    n_mem = mem_prompt.shape[1]
    n_experts = w_router.shape[2]
    assert LANES % n_experts == 0 and 3 * n_experts <= LANES
    assert ATT_WIDTH % MXU_DIM == 0 and (X_HEADS * X_HEAD_DIM) % MXU_DIM == 0

    slopes = jnp.asarray([2.0 ** (-8.0 * (h + 1) / N_HEADS) for h in range(N_HEADS)], F32)
    grp = jnp.arange(MXU_DIM) // HEAD_DIM
    gmat = (grp[:, None] == grp[None, :]).astype(BF16)
    row = lambda a: a.reshape(1, -1).astype(F32)

    hp, hs = x_prompt, x_sample
    outs = [[] for _ in range(8)]
    for l in range(depth):
        lam_init = 0.8 - 0.6 * math.exp(-0.3 * l)
        w_in_b = w_in[l].astype(BF16)
        w_out_b = w_out[l].astype(BF16)
        w_xq_b = w_xq[l].astype(BF16)
        w_xkv_b = w_xkv[l].astype(BF16)
        w_xo_b = w_xo[l].astype(BF16)
        gqk = jnp.concatenate([jnp.tile(g_q[l].reshape(-1), N_HEADS), jnp.tile(g_k[l].reshape(-1), N_HEADS)])[None, :]
        gxq = (jnp.tile(g_xq[l], X_HEADS) * (X_HEAD_DIM ** -0.5))[None, :]
        gxk = jnp.tile(g_xk[l], X_HEADS)[None, :]
        wbd = jax.scipy.linalg.block_diag(*[w_pool[l, g] for g in range(len(POOL_WINDOWS))]).astype(BF16)
        wr_pad_b = jnp.pad(w_router[l], ((0, 0), (0, LANES - n_experts))).astype(BF16)
        br_pad = jnp.pad(b_router[l].astype(F32), (0, LANES - n_experts), constant_values=NEG)[None, :]
        moe_w = (row(g_ffn[l]), wr_pad_b, br_pad, w_gate_up[l], b_gate_up[l], w_down[l], b_down[l])
        gsub = row(g_sub[l])

        q_b, k_b, v_t, k32, v32, up = _proj(hp.reshape(b_p * s_p, d), row(g_mix[l]), w_in_b, gqk, gmat)
        o_att = _attn(q_b, k_b, v_t, slopes, lam[l], g_sub[l].reshape(-1, 1).astype(F32), s_p, lam_init)
        up3 = up.reshape(b_p, s_p, -1)
        h1 = _outproj(hp, o_att.reshape(b_p, s_p, -1), up3, up3, wbd, row(pool_scale[l]), w_out_b, n_prev=0,
                      halo_from_up=True)
        mk, mv = _memkv(mem_prompt.reshape(b_p * n_mem, d), row(g_mem[l]), w_xkv_b, gxk, gmat)
        h2 = _xattn(h1, mk.reshape(b_p, n_mem, -1), mv.reshape(b_p, n_mem, -1), row(g_xattn[l]), w_xq_b, gxq, gmat,
                    w_xo_b)
        outs[0].append(k32.reshape(b_p, s_p, N_HEADS, V_DIM))
        outs[1].append(v32.reshape(b_p, s_p, N_HEADS, V_DIM))
        outs[2].append(up3[:, s_p - POOL_BUF:])
        outs[3].append(mk.reshape(b_p, n_mem, X_HEADS, X_HEAD_DIM))
        outs[4].append(mv.reshape(b_p, n_mem, X_HEADS, X_HEAD_DIM))

        q_s, _, _, k32_s, v32_s, up_s = _proj(hs.reshape(b_s * t_s, d), row(g_mix[l]), w_in_b, gqk, gmat)
        kn = k32_s.reshape(b_s, t_s, N_HEADS, V_DIM)
        vn = v32_s.reshape(b_s, t_s, N_HEADS, V_DIM)
        o_att_s = _sample_attn(q_s.reshape(b_s, t_s, -1), kn, vn, cache_k, cache_v, l, page_table, slopes, lam[l],
                               gsub, lam_init)
        ups = up_s.reshape(b_s, t_s, -1)
        halo_s = jnp.pad(state_pool[l], ((0, 0), (HALO - POOL_BUF, 0), (0, 0)))
        h1s = _outproj(hs, o_att_s, ups, halo_s, wbd, row(pool_scale[l]), w_out_b, n_prev=POOL_BUF, halo_from_up=False)
        h2s = _xattn(h1s, cache_mem_k[l].reshape(b_s, n_mem, -1), cache_mem_v[l].reshape(b_s, n_mem, -1),
                     row(g_xattn[l]), w_xq_b, gxq, gmat, w_xo_b)
        hp, hs = _moe([h2, h2s], *moe_w)
        outs[5].append(kn.reshape(b_s, t_s, N_HEADS, V_DIM))
        outs[6].append(vn.reshape(b_s, t_s, N_HEADS, V_DIM))
        outs[7].append(jnp.concatenate([state_pool[l], ups], axis=1)[:, t_s:])

    stacked = [jnp.stack(o) for o in outs]
    return (hp, hs, *stacked)
```
